```python
import math
import jax, jax.numpy as jnp
from jax import lax
import numpy as np

D_MODEL = 2048
BATCH = 16
SEQ = 2048
DEPTH = 2

D_MIX = D_MODEL
GLA_WIDTH = D_MIX // 2
DIFF_WIDTH = D_MIX - GLA_WIDTH
GLA_HEADS = 4
GLA_KEY_WIDTH = GLA_WIDTH // 2
GLA_HEAD_K = GLA_KEY_WIDTH // GLA_HEADS
GLA_HEAD_V = GLA_WIDTH // GLA_HEADS
GLA_GATE_RANK = 16
GLA_GATE_TAU = 16.0
GLA_CHUNK = 64
DIFF_HEAD_DIM = 128
DIFF_HEADS = DIFF_WIDTH // (2 * DIFF_HEAD_DIM)
Q_BLOCK = 128
ROPE_THETA = 10000.0
D_FF = 4 * D_MODEL
NORM_EPS = 1e-6
SUBLN_EPS = 1e-5

IN_SIZES = (GLA_KEY_WIDTH, GLA_KEY_WIDTH, GLA_WIDTH, GLA_WIDTH, GLA_GATE_RANK,
            DIFF_HEADS * 2 * DIFF_HEAD_DIM, DIFF_HEADS * 2 * DIFF_HEAD_DIM, DIFF_WIDTH)
IN_COLS = sum(IN_SIZES)

kernel_name = "hymba_style_gla_diffattn_hybrid"


def rms_norm(x, w, eps):
    xf = x.astype(jnp.float32)
    y = xf * lax.rsqrt(jnp.mean(xf * xf, axis=-1, keepdims=True) + eps)
    return (y * w.astype(jnp.float32)).astype(x.dtype)


def rope_tables(positions, dim):
    inv_freq = ROPE_THETA ** (-jnp.arange(0, dim, 2, dtype=jnp.float32) / dim)
    ang = positions.astype(jnp.float32)[..., None] * inv_freq
    return jnp.cos(ang), jnp.sin(ang)


def apply_rope(t, cos, sin):
    half = t.shape[-1] // 2
    t1, t2 = t[..., :half], t[..., half:]
    c, s = cos.astype(t.dtype), sin.astype(t.dtype)
    return jnp.concatenate([t1 * c - t2 * s, t1 * s + t2 * c], axis=-1)


def lambda_init_fn(layer_idx):
    return 0.8 - 0.6 * math.exp(-0.3 * layer_idx)


def gla_mixer(q, k, v, g, gate_lr, gate_w2, gate_b, out_norm_w):
    B, S, _ = q.shape
    H, dk, dv, C = GLA_HEADS, GLA_HEAD_K, GLA_HEAD_V, GLA_CHUNK
    N = S // C
    log_alpha = jax.nn.log_sigmoid((gate_lr @ gate_w2 + gate_b).astype(jnp.float32)) / GLA_GATE_TAU

    def chunks(t, d):
        return t.reshape(B, N, C, H, d).transpose(1, 0, 3, 2, 4)

    qc = chunks(q * (dk ** -0.5), dk)
    kc = chunks(k, dk)
    vc = chunks(v, dv)
    bc = jnp.cumsum(chunks(log_alpha, dk), axis=3)
    causal = jnp.tril(jnp.ones((C, C), dtype=bool))[:, :, None]

    def step(state, inp):
        q_i, k_i, v_i, b_i = inp
        o_inter = jnp.einsum('bhcd,bhdv->bhcv', q_i * jnp.exp(b_i), state)
        rel = b_i[:, :, :, None, :] - b_i[:, :, None, :, :]
        decay = jnp.exp(jnp.where(causal, rel, -jnp.inf))
        attn = jnp.einsum('bhid,bhjd,bhijd->bhij', q_i, k_i, decay)
        o_intra = jnp.einsum('bhij,bhjv->bhiv', attn, v_i)
        b_last = b_i[:, :, -1:, :]
        k_dec = k_i * jnp.exp(b_last - b_i)
        new_state = (state * jnp.exp(b_last)[:, :, 0, :, None]
                     + jnp.einsum('bhcd,bhcv->bhdv', k_dec, v_i)).astype(jnp.float32)
        return new_state, (o_inter + o_intra).astype(jnp.float32)

    state0 = jnp.zeros((B, H, dk, dv), dtype=jnp.float32)
    _, o = lax.scan(step, state0, (qc, kc, vc, bc))
    o = o.transpose(1, 0, 3, 2, 4).reshape(B, S, H, dv)
    o = rms_norm(o, out_norm_w, NORM_EPS) * jax.nn.silu(g.reshape(B, S, H, dv).astype(jnp.float32))
    return o.reshape(B, S, H * dv).astype(v.dtype)


def diff_attention(q, k, v, cos, sin, q_norm_w, k_norm_w, lam_q1, lam_k1, lam_q2, lam_k2,
                   subln_w, lam_init):
    B, S, _ = q.shape
    H, dh = DIFF_HEADS, DIFF_HEAD_DIM
    q = rms_norm(q.reshape(B, S, H, 2, dh), q_norm_w, NORM_EPS).transpose(0, 2, 3, 1, 4)
    k = rms_norm(k.reshape(B, S, H, 2, dh), k_norm_w, NORM_EPS).transpose(0, 2, 3, 1, 4)
    q = apply_rope(q, cos, sin)
    k = apply_rope(k, cos, sin)
    v = v.reshape(B, S, H, 2 * dh).transpose(0, 2, 1, 3)
    f32 = jnp.float32
    lam = (jnp.exp(jnp.sum(lam_q1.astype(f32) * lam_k1.astype(f32)))
           - jnp.exp(jnp.sum(lam_q2.astype(f32) * lam_k2.astype(f32))) + lam_init)
    scale = dh ** -0.5
    outs = []
    for blk in range(S // Q_BLOCK):
        start = blk * Q_BLOCK
        end = start + Q_BLOCK
        s = jnp.einsum('bhcqd,bhckd->bhcqk', q[:, :, :, start:end], k[:, :, :, :end]).astype(f32) * scale
        mask = (start + jnp.arange(Q_BLOCK))[:, None] >= jnp.arange(end)[None, :]
        p = jax.nn.softmax(jnp.where(mask, s, -jnp.inf), axis=-1)
        w = p[:, :, 0] - lam * p[:, :, 1]
        outs.append(jnp.einsum('bhqk,bhkd->bhqd', w.astype(v.dtype), v[:, :, :end]))
    o = jnp.concatenate(outs, axis=2)
    o = rms_norm(o, subln_w, SUBLN_EPS) * (1.0 - lam_init)
    return o.transpose(0, 2, 1, 3).reshape(B, S, H * 2 * dh)


def setup_inputs(seed: int = 0) -> dict:
    key = jax.random.key(seed)
    ks = jax.random.split(key, 20)

    def nrm(k, shape, scale):
        return jax.random.normal(k, shape, dtype=jnp.float32) * scale

    x = nrm(ks[0], (BATCH, SEQ, D_MODEL), 1.0)
    positions = (jnp.arange(SEQ, dtype=jnp.int32)[None, :]
                 + jax.random.randint(ks[1], (BATCH, 1), 0, 1024, dtype=jnp.int32))
    return {
        "x": x,
        "positions": positions,
        "attn_norm_w": 1.0 + nrm(ks[2], (DEPTH, D_MODEL), 0.02),
        "w_in": nrm(ks[3], (DEPTH, D_MODEL, IN_COLS), D_MODEL ** -0.5),
        "gla_gate_w2": nrm(ks[4], (DEPTH, GLA_GATE_RANK, GLA_KEY_WIDTH), GLA_GATE_RANK ** -0.5),
        "gla_gate_b": nrm(ks[5], (DEPTH, GLA_KEY_WIDTH), 0.1),
        "gla_out_norm_w": 1.0 + nrm(ks[6], (DEPTH, GLA_HEAD_V), 0.02),
        "diff_q_norm_w": 1.0 + nrm(ks[7], (DEPTH, DIFF_HEAD_DIM), 0.02),
        "diff_k_norm_w": 1.0 + nrm(ks[8], (DEPTH, DIFF_HEAD_DIM), 0.02),
        "diff_lambda_q1": nrm(ks[9], (DEPTH, DIFF_HEAD_DIM), 0.1),
        "diff_lambda_k1": nrm(ks[10], (DEPTH, DIFF_HEAD_DIM), 0.1),
        "diff_lambda_q2": nrm(ks[11], (DEPTH, DIFF_HEAD_DIM), 0.1),
        "diff_lambda_k2": nrm(ks[12], (DEPTH, DIFF_HEAD_DIM), 0.1),
        "diff_subln_w": 1.0 + nrm(ks[13], (DEPTH, 2 * DIFF_HEAD_DIM), 0.02),
        "w_out": nrm(ks[14], (DEPTH, D_MIX, D_MODEL), D_MIX ** -0.5),
        "mlp_norm_w": 1.0 + nrm(ks[15], (DEPTH, D_MODEL), 0.02),
        "w_up": nrm(ks[16], (DEPTH, D_MODEL, D_FF), D_MODEL ** -0.5),
        "w_down": nrm(ks[17], (DEPTH, D_FF, D_MODEL), D_FF ** -0.5),
    }


def reference(x, positions, attn_norm_w, w_in, gla_gate_w2, gla_gate_b, gla_out_norm_w,
              diff_q_norm_w, diff_k_norm_w, diff_lambda_q1, diff_lambda_k1, diff_lambda_q2,
              diff_lambda_k2, diff_subln_w, w_out, mlp_norm_w, w_up, w_down):
    cos, sin = rope_tables(positions, DIFF_HEAD_DIM)
    cos = cos[:, None, None]
    sin = sin[:, None, None]
    split_points = np.cumsum(IN_SIZES)[:-1].tolist()
    for layer in range(DEPTH):
        n = rms_norm(x, attn_norm_w[layer], NORM_EPS)
        proj = n @ w_in[layer]
        g_q, g_k, g_v, g_g, g_lr, d_q, d_k, d_v = jnp.split(proj, split_points, axis=-1)
        y_gla = gla_mixer(g_q, g_k, g_v, g_g, g_lr, gla_gate_w2[layer], gla_gate_b[layer],
                          gla_out_norm_w[layer])
        y_diff = diff_attention(d_q, d_k, d_v, cos, sin, diff_q_norm_w[layer], diff_k_norm_w[layer],
                                diff_lambda_q1[layer], diff_lambda_k1[layer],
                                diff_lambda_q2[layer], diff_lambda_k2[layer],
                                diff_subln_w[layer], lambda_init_fn(layer))
        mixed = jnp.concatenate([y_gla, y_diff], axis=-1)
        x = x + mixed @ w_out[layer]
        n = rms_norm(x, mlp_norm_w[layer], NORM_EPS)
        x = x + jnp.square(jax.nn.relu(n @ w_up[layer])) @ w_down[layer]
    return x
```

```python
import functools
import math

import numpy as np
import jax
import jax.numpy as jnp
from jax import lax
from jax.experimental import pallas as pl
from jax.experimental.pallas import tpu as pltpu

F32 = jnp.float32
BF16 = jnp.bfloat16

GLA_HEADS = 4
GLA_HEAD_K = 128
GLA_HEAD_V = 256
GLA_KEY_WIDTH = GLA_HEADS * GLA_HEAD_K
GLA_WIDTH = GLA_HEADS * GLA_HEAD_V
GLA_GATE_RANK = 16
GLA_GATE_TAU = 16.0
DIFF_HEADS = 4
DIFF_HEAD_DIM = 128
DIFF_WIDTH = DIFF_HEADS * 2 * DIFF_HEAD_DIM
ROPE_THETA = 10000.0
NORM_EPS = 1e-6
SUBLN_EPS = 1e-5
LOG2E = math.log2(math.e)

V7X_LANES = 128
V7X_VMEM_BYTES = 64 * 1024 * 1024
VMEM_LIMIT_BYTES = 56 * 1024 * 1024

COL_GQ, COL_GK, COL_GV, COL_GG = 0, 512, 1024, 2048
COL_DQ, COL_DK, COL_DV = 3072, 4096, 5120
MAIN_COLS = 6144

GLA_CHUNK = 256
GLA_LEVELS = int(math.log2(GLA_CHUNK))
Q_TILE = 256


def _cparams(sem):
    return pltpu.CompilerParams(dimension_semantics=sem, vmem_limit_bytes=VMEM_LIMIT_BYTES)


def _in_proj_kernel(x_ref, nw_ref, w_ref, wlr_ref, o_ref, olr_ref, n_ref):
    @pl.when(pl.program_id(1) == 0)
    def _():
        x = x_ref[...]
        ms = jnp.mean(x * x, axis=-1, keepdims=True)
        n = (x * lax.rsqrt(ms + NORM_EPS) * nw_ref[...]).astype(BF16)
        n_ref[...] = n
        olr_ref[...] = jnp.dot(n, wlr_ref[...], preferred_element_type=F32).astype(BF16)

    o_ref[...] = jnp.dot(n_ref[...], w_ref[...], preferred_element_type=F32).astype(BF16)


def _in_proj(x2, norm_w, w_main, w_lr, tm, tn):
    m, d = x2.shape
    return pl.pallas_call(
        _in_proj_kernel,
        grid=(m // tm, MAIN_COLS // tn),
        in_specs=[
            pl.BlockSpec((tm, d), lambda i, j: (i, 0)),
            pl.BlockSpec((1, d), lambda i, j: (0, 0)),
            pl.BlockSpec((d, tn), lambda i, j: (0, j)),
            pl.BlockSpec((d, V7X_LANES), lambda i, j: (0, 0)),
        ],
        out_specs=[
            pl.BlockSpec((tm, tn), lambda i, j: (i, j)),
            pl.BlockSpec((tm, V7X_LANES), lambda i, j: (i, 0)),
        ],
        out_shape=[
            jax.ShapeDtypeStruct((m, MAIN_COLS), BF16),
            jax.ShapeDtypeStruct((m, V7X_LANES), BF16),
        ],
        scratch_shapes=[pltpu.VMEM((tm, d), BF16)],
        compiler_params=_cparams(("parallel", "arbitrary")),
        name="in_proj",
    )(x2, norm_w, w_main, w_lr)


def _gla_constants():
    c = GLA_CHUNK
    i = np.arange(c)[:, None]
    t = np.arange(c)[None, :]
    mats = [(t <= i), (t > i)]
    for lvl in range(GLA_LEVELS):
        s = 1 << lvl
        m = (i // (2 * s)) * (2 * s) + s - 1
        mats.append((t > np.minimum(i, m)) & (t <= np.maximum(i, m)))
    wstack = np.concatenate(mats, axis=0).astype(np.float32)
    x = i ^ t
    level = np.where(i > t, np.floor(np.log2(np.maximum(x, 1))), np.where(i == t, -1, -2))
    return wstack, level.astype(np.int32)


def _gla_kernel(q_ref, k_ref, v_ref, g_ref, lr_ref, w2_ref, gb_ref, wstk_ref, lvl_ref, onw_ref,
                o_ref, state_ref, *, seq):
    c = GLA_CHUNK
    dk = GLA_HEAD_K
    state_ref[...] = jnp.zeros_like(state_ref)
    w2 = w2_ref[...]
    gb = gb_ref[...]
    onw = onw_ref[...]
    lvl = lvl_ref[...]
    row = lax.broadcasted_iota(jnp.int32, (c, 1), 0)
    eye = (lax.broadcasted_iota(jnp.int32, (dk, dk), 0)
           == lax.broadcasted_iota(jnp.int32, (dk, dk), 1))
    nt = (((1,), (1,)), ((), ()))
    tn = (((0,), (0,)), ((), ()))

    def chunk(ci, carry):
        r0 = pl.multiple_of(ci * c, c)
        qs = q_ref[pl.ds(r0, c), :].astype(F32) * (dk ** -0.5)
        kf = k_ref[pl.ds(r0, c), :].astype(F32)
        kb = k_ref[pl.ds(r0, c), :]
        vb = v_ref[pl.ds(r0, c), :]
        z = jnp.dot(lr_ref[pl.ds(r0, c), :], w2, preferred_element_type=F32) + gb
        la = (jnp.minimum(z, 0.0) - jnp.log1p(jnp.exp(-jnp.abs(z)))) * (1.0 / GLA_GATE_TAU)
        la_hi = la.astype(BF16)
        la_lo = (la - la_hi.astype(F32)).astype(BF16)
        sums = jnp.dot(wstk_ref[...], jnp.concatenate([la_hi, la_lo], axis=1),
                       preferred_element_type=F32)
        sums = sums[:, :dk] + sums[:, dk:]

        b = sums[0:c]
        b_rev = sums[c:2 * c]
        state = state_ref[...]
        o = jnp.dot((qs * jnp.exp(b)).astype(BF16), state.astype(BF16),
                    preferred_element_type=F32)

        attn = jnp.where(lvl == -1,
                         lax.dot_general(qs.astype(BF16), kb, nt, preferred_element_type=F32), 0.0)
        for l in range(GLA_LEVELS):
            e = jnp.exp(sums[(l + 2) * c:(l + 3) * c])
            upper = ((row >> l) & 1) == 1
            qt = jnp.where(upper, qs * e, 0.0).astype(BF16)
            kt = jnp.where(upper, 0.0, kf * e).astype(BF16)
            p = lax.dot_general(qt, kt, nt, preferred_element_type=F32)
            attn = jnp.where(lvl == l, p, attn)
        o = o + jnp.dot(attn.astype(BF16), vb, preferred_element_type=F32)

        g = g_ref[pl.ds(r0, c), :].astype(F32)
        y = o * lax.rsqrt(jnp.mean(o * o, axis=-1, keepdims=True) + NORM_EPS) * onw
        y = y * (g * jax.nn.sigmoid(g))
        o_ref[pl.ds(r0, c), :] = y.astype(o_ref.dtype)

        e_last = jnp.exp(b[c - 1:c, :])
        e_col = jnp.sum(jnp.where(eye, e_last, 0.0), axis=1, keepdims=True)
        kd = (kf * jnp.exp(b_rev)).astype(BF16)
        state_ref[...] = state * e_col + lax.dot_general(kd, vb, tn, preferred_element_type=F32)
        return carry

    lax.fori_loop(0, seq // c, chunk, 0)


def _gla(proj, lr, w2p, gate_b, wstack, level, out_norm_w, batch, seq):
    m = proj.shape[0]
    kq = GLA_HEAD_K
    kv = GLA_HEAD_V
    rows = wstack.shape[0]
    return pl.pallas_call(
        functools.partial(_gla_kernel, seq=seq),
        grid=(batch, GLA_HEADS),
        in_specs=[
            pl.BlockSpec((seq, kq), lambda b, h: (b, COL_GQ // kq + h)),
            pl.BlockSpec((seq, kq), lambda b, h: (b, COL_GK // kq + h)),
            pl.BlockSpec((seq, kv), lambda b, h: (b, COL_GV // kv + h)),
            pl.BlockSpec((seq, kv), lambda b, h: (b, COL_GG // kv + h)),
            pl.BlockSpec((seq, V7X_LANES), lambda b, h: (b, 0)),
            pl.BlockSpec((V7X_LANES, kq), lambda b, h: (0, h)),
            pl.BlockSpec((1, kq), lambda b, h: (0, h)),
            pl.BlockSpec((rows, GLA_CHUNK), lambda b, h: (0, 0)),
            pl.BlockSpec((GLA_CHUNK, GLA_CHUNK), lambda b, h: (0, 0)),
            pl.BlockSpec((1, kv), lambda b, h: (0, 0)),
        ],
        out_specs=pl.BlockSpec((seq, kv), lambda b, h: (b, h)),
        out_shape=jax.ShapeDtypeStruct((m, GLA_WIDTH), BF16),
        scratch_shapes=[pltpu.VMEM((kq, kv), F32)],
        compiler_params=_cparams(("parallel", "parallel")),
        name="gla",
    )(proj, proj, proj, proj, lr, w2p, gate_b, wstack, level, out_norm_w)


def _diff_kernel(q_ref, k_ref, v_ref, cos_ref, sin_ref, qnw_ref, knw_ref,
                 lq1_ref, lk1_ref, lq2_ref, lk2_ref, sw_ref, o_ref, qs_ref, ks_ref,
                 *, seq, lam_init):
    dh = DIFF_HEAD_DIM
    tq = Q_TILE
    lam = (jnp.exp(jnp.sum(lq1_ref[...] * lk1_ref[...], axis=-1, keepdims=True))
           - jnp.exp(jnp.sum(lq2_ref[...] * lk2_ref[...], axis=-1, keepdims=True))
           + lam_init)
    cosf = cos_ref[...]
    sinf = sin_ref[...]
    qscale = (dh ** -0.5) * LOG2E

    def prep(src_ref, w_ref, dst_ref, scale):
        for comp in range(2):
            t = src_ref[:, comp * dh:(comp + 1) * dh].astype(F32)
            t = t * lax.rsqrt(jnp.mean(t * t, axis=-1, keepdims=True) + NORM_EPS) * w_ref[...]
            t = t * cosf + pltpu.roll(t, dh // 2, 1) * sinf
            dst_ref[comp] = (t * scale).astype(BF16)

    prep(q_ref, qnw_ref, qs_ref, qscale)
    prep(k_ref, knw_ref, ks_ref, 1.0)

    nt = (((1,), (1,)), ((), ()))
    tri = (lax.broadcasted_iota(jnp.int32, (tq, tq), 0)
           >= lax.broadcasted_iota(jnp.int32, (tq, tq), 1))
    sw = sw_ref[...]
    for t in range(seq // tq):
        r0 = t * tq
        ws = []
        for comp in range(2):
            q = qs_ref[comp, r0:r0 + tq, :]
            s = lax.dot_general(q, ks_ref[comp, r0:r0 + tq, :], nt, preferred_element_type=F32)
            s = jnp.where(tri, s, -jnp.inf)
            if t > 0:
                s_off = lax.dot_general(q, ks_ref[comp, 0:r0, :], nt, preferred_element_type=F32)
                s = jnp.concatenate([s_off, s], axis=1)
            p = jnp.exp2(s - jnp.max(s, axis=-1, keepdims=True))
            ws.append(p * (1.0 / jnp.sum(p, axis=-1, keepdims=True)))
        w = (ws[0] - lam * ws[1]).astype(BF16)
        o = jnp.dot(w, v_ref[0:r0 + tq, :], preferred_element_type=F32)
        o = o * lax.rsqrt(jnp.mean(o * o, axis=-1, keepdims=True) + SUBLN_EPS) * sw
        o_ref[r0:r0 + tq, :] = (o * (1.0 - lam_init)).astype(o_ref.dtype)


def _diff_attn(proj, cosf, sinf, qnw, knw, lq1, lk1, lq2, lk2, subln_w, batch, seq, lam_init):
    m = proj.shape[0]
    hw = 2 * DIFF_HEAD_DIM
    vec = pl.BlockSpec((1, DIFF_HEAD_DIM), lambda b, h: (0, 0))
    return pl.pallas_call(
        functools.partial(_diff_kernel, seq=seq, lam_init=lam_init),
        grid=(batch, DIFF_HEADS),
        in_specs=[
            pl.BlockSpec((seq, hw), lambda b, h: (b, COL_DQ // hw + h)),
            pl.BlockSpec((seq, hw), lambda b, h: (b, COL_DK // hw + h)),
            pl.BlockSpec((seq, hw), lambda b, h: (b, COL_DV // hw + h)),
            pl.BlockSpec((seq, DIFF_HEAD_DIM), lambda b, h: (b, 0)),
            pl.BlockSpec((seq, DIFF_HEAD_DIM), lambda b, h: (b, 0)),
            vec, vec, vec, vec, vec, vec,
            pl.BlockSpec((1, hw), lambda b, h: (0, 0)),
        ],
        out_specs=pl.BlockSpec((seq, hw), lambda b, h: (b, h)),
        out_shape=jax.ShapeDtypeStruct((m, DIFF_WIDTH), BF16),
        scratch_shapes=[pltpu.VMEM((2, seq, DIFF_HEAD_DIM), BF16),
                        pltpu.VMEM((2, seq, DIFF_HEAD_DIM), BF16)],
        compiler_params=_cparams(("parallel", "parallel")),
        name="diff_attn",
    )(proj, proj, proj, cosf, sinf, qnw, knw, lq1, lk1, lq2, lk2, subln_w)


def _out_proj_kernel(x_ref, yg_ref, yd_ref, wg_ref, wd_ref, o_ref):
    acc = jnp.dot(yg_ref[...], wg_ref[...], preferred_element_type=F32)
    acc = acc + jnp.dot(yd_ref[...], wd_ref[...], preferred_element_type=F32)
    o_ref[...] = x_ref[...] + acc


def _out_proj(x2, yg, yd, wg, wd, tm):
    m, d = x2.shape
    return pl.pallas_call(
        _out_proj_kernel,
        grid=(m // tm,),
        in_specs=[
            pl.BlockSpec((tm, d), lambda i: (i, 0)),
            pl.BlockSpec((tm, GLA_WIDTH), lambda i: (i, 0)),
            pl.BlockSpec((tm, DIFF_WIDTH), lambda i: (i, 0)),
            pl.BlockSpec((GLA_WIDTH, d), lambda i: (0, 0)),
            pl.BlockSpec((DIFF_WIDTH, d), lambda i: (0, 0)),
        ],
        out_specs=pl.BlockSpec((tm, d), lambda i: (i, 0)),
        out_shape=jax.ShapeDtypeStruct((m, d), F32),
        compiler_params=_cparams(("parallel",)),
        name="out_proj",
    )(x2, yg, yd, wg, wd)


def _mlp_kernel(x_ref, nw_ref, wu_ref, wd_ref, o_ref, n_ref):
    @pl.when(pl.program_id(1) == 0)
    def _():
        x = x_ref[...]
        ms = jnp.mean(x * x, axis=-1, keepdims=True)
        n_ref[...] = (x * lax.rsqrt(ms + NORM_EPS) * nw_ref[...]).astype(BF16)
        o_ref[...] = x

    h = jnp.dot(n_ref[...], wu_ref[...], preferred_element_type=F32)
    h = jnp.square(jnp.maximum(h, 0.0)).astype(BF16)
    o_ref[...] += jnp.dot(h, wd_ref[...], preferred_element_type=F32)


def _mlp(x2, norm_w, w_up, w_down, tm, tf):
    m, d = x2.shape
    f = w_up.shape[1]
    return pl.pallas_call(
        _mlp_kernel,
        grid=(m // tm, f // tf),
        in_specs=[
            pl.BlockSpec((tm, d), lambda i, j: (i, 0)),
            pl.BlockSpec((1, d), lambda i, j: (0, 0)),
            pl.BlockSpec((d, tf), lambda i, j: (0, j)),
            pl.BlockSpec((tf, d), lambda i, j: (j, 0)),
        ],
        out_specs=pl.BlockSpec((tm, d), lambda i, j: (i, 0)),
        out_shape=jax.ShapeDtypeStruct((m, d), F32),
        scratch_shapes=[pltpu.VMEM((tm, d), BF16)],
        compiler_params=_cparams(("parallel", "arbitrary")),
        name="mlp",
    )(x2, norm_w, w_up, w_down)


def _lambda_init(layer_idx):
    return 0.8 - 0.6 * math.exp(-0.3 * layer_idx)


def kernel(x, positions, attn_norm_w, w_in, gla_gate_w2, gla_gate_b, gla_out_norm_w,
           diff_q_norm_w, diff_k_norm_w, diff_lambda_q1, diff_lambda_k1, diff_lambda_q2,
           diff_lambda_k2, diff_subln_w, w_out, mlp_norm_w, w_up, w_down):
    batch, seq, d = x.shape
    depth = w_in.shape[0]
    m = batch * seq
    assert d == GLA_WIDTH + DIFF_WIDTH and seq % GLA_CHUNK == 0 and seq % Q_TILE == 0
    tm_proj = min(1024, m)
    tm_out = min(512, m)
    tm_mlp = min(512, m)

    half = DIFF_HEAD_DIM // 2
    inv_freq = ROPE_THETA ** (-jnp.arange(0, DIFF_HEAD_DIM, 2, dtype=F32) / DIFF_HEAD_DIM)
    ang = positions.astype(F32)[..., None] * inv_freq
    cos, sin = jnp.cos(ang), jnp.sin(ang)
    cosf = jnp.concatenate([cos, cos], axis=-1).reshape(m, 2 * half)
    sinf = jnp.concatenate([-sin, sin], axis=-1).reshape(m, 2 * half)

    wstack_np, level_np = _gla_constants()
    wstack = jnp.asarray(wstack_np, dtype=BF16)
    level = jnp.asarray(level_np)

    sizes = (GLA_KEY_WIDTH, GLA_KEY_WIDTH, GLA_WIDTH, GLA_WIDTH, GLA_GATE_RANK,
             DIFF_WIDTH, DIFF_WIDTH, DIFF_WIDTH)
    offs = np.concatenate([[0], np.cumsum(sizes)])
    lr0, lr1 = int(offs[4]), int(offs[5])

    x2 = x.reshape(m, d)
    for layer in range(depth):
        wl = w_in[layer]
        w_main = jnp.concatenate([wl[:, :lr0], wl[:, lr1:]], axis=1).astype(BF16)
        w_lr = jnp.pad(wl[:, lr0:lr1], ((0, 0), (0, V7X_LANES - GLA_GATE_RANK))).astype(BF16)
        w2p = jnp.pad(gla_gate_w2[layer], ((0, V7X_LANES - GLA_GATE_RANK), (0, 0))).astype(BF16)

        proj, lr = _in_proj(x2, attn_norm_w[layer][None, :], w_main, w_lr, tm_proj, 1024)
        y_gla = _gla(proj, lr, w2p, gla_gate_b[layer][None, :], wstack, level,
                     gla_out_norm_w[layer][None, :], batch, seq)
        y_diff = _diff_attn(proj, cosf, sinf,
                            diff_q_norm_w[layer][None, :], diff_k_norm_w[layer][None, :],
                            diff_lambda_q1[layer][None, :], diff_lambda_k1[layer][None, :],
                            diff_lambda_q2[layer][None, :], diff_lambda_k2[layer][None, :],
                            diff_subln_w[layer][None, :], batch, seq, _lambda_init(layer))
        wo = w_out[layer].astype(BF16)
        x2 = _out_proj(x2, y_gla, y_diff, wo[:GLA_WIDTH], wo[GLA_WIDTH:], tm_out)
        x2 = _mlp(x2, mlp_norm_w[layer][None, :], w_up[layer].astype(BF16),
                  w_down[layer].astype(BF16), tm_mlp, 1024)
    return x2.reshape(batch, seq, d)
```

```python
import functools
import math

import numpy as np
import jax
import jax.numpy as jnp
from jax import lax
from jax.experimental import pallas as pl
from jax.experimental.pallas import tpu as pltpu

F32 = jnp.float32
BF16 = jnp.bfloat16

GLA_HEADS = 4
GLA_HEAD_K = 128
GLA_HEAD_V = 256
GLA_KEY_WIDTH = GLA_HEADS * GLA_HEAD_K
GLA_WIDTH = GLA_HEADS * GLA_HEAD_V
GLA_GATE_RANK = 16
GLA_GATE_TAU = 16.0
DIFF_HEADS = 4
DIFF_HEAD_DIM = 128
DIFF_WIDTH = DIFF_HEADS * 2 * DIFF_HEAD_DIM
ROPE_THETA = 10000.0
NORM_EPS = 1e-6
SUBLN_EPS = 1e-5
LOG2E = math.log2(math.e)

V7X_LANES = 128
V7X_VMEM_BYTES = 64 * 1024 * 1024
VMEM_LIMIT_BYTES = 56 * 1024 * 1024

COL_GQ, COL_GK, COL_GV, COL_GG = 0, 512, 1024, 2048
COL_DQ, COL_DK, COL_DV = 3072, 4096, 5120
MAIN_COLS = 6144

GLA_CHUNK = 256
GLA_LEVELS = int(math.log2(GLA_CHUNK))
GLA_FAST_MAX_DECAY = 64.0
Q_TILE = 256
SCORE_BOUND_NO_SHIFT = 60.0


def _cparams(sem):
    return pltpu.CompilerParams(dimension_semantics=sem, vmem_limit_bytes=VMEM_LIMIT_BYTES)


def _in_proj_kernel(x_ref, nw_ref, w_ref, wlr_ref, o_ref, olr_ref, n_ref):
    @pl.when(pl.program_id(1) == 0)
    def _():
        x = x_ref[...]
        ms = jnp.mean(x * x, axis=-1, keepdims=True)
        n = (x * lax.rsqrt(ms + NORM_EPS) * nw_ref[...]).astype(BF16)
        n_ref[...] = n
        olr_ref[...] = jnp.dot(n, wlr_ref[...], preferred_element_type=F32).astype(BF16)

    o_ref[...] = jnp.dot(n_ref[...], w_ref[...], preferred_element_type=F32).astype(BF16)


def _in_proj(x2, norm_w, w_main, w_lr, tm, tn):
    m, d = x2.shape
    return pl.pallas_call(
        _in_proj_kernel,
        grid=(m // tm, MAIN_COLS // tn),
        in_specs=[
            pl.BlockSpec((tm, d), lambda i, j: (i, 0)),
            pl.BlockSpec((1, d), lambda i, j: (0, 0)),
            pl.BlockSpec((d, tn), lambda i, j: (0, j)),
            pl.BlockSpec((d, V7X_LANES), lambda i, j: (0, 0)),
        ],
        out_specs=[
            pl.BlockSpec((tm, tn), lambda i, j: (i, j)),
            pl.BlockSpec((tm, V7X_LANES), lambda i, j: (i, 0)),
        ],
        out_shape=[
            jax.ShapeDtypeStruct((m, MAIN_COLS), BF16),
            jax.ShapeDtypeStruct((m, V7X_LANES), BF16),
        ],
        scratch_shapes=[pltpu.VMEM((tm, d), BF16)],
        compiler_params=_cparams(("parallel", "arbitrary")),
        name="in_proj",
    )(x2, norm_w, w_main, w_lr)


def _gla_constants():
    c = GLA_CHUNK
    i = np.arange(c)[:, None]
    t = np.arange(c)[None, :]
    mats = [(t <= i), (t > i)]
    for lvl in range(GLA_LEVELS):
        s = 1 << lvl
        m = (i // (2 * s)) * (2 * s) + s - 1
        mats.append((t > np.minimum(i, m)) & (t <= np.maximum(i, m)))
    wstack = np.concatenate(mats, axis=0).astype(np.float32)
    x = i ^ t
    level = np.where(i > t, np.floor(np.log2(np.maximum(x, 1))), np.where(i == t, -1, -2))
    return wstack, level.astype(np.int32)


def _gla_kernel(q_ref, k_ref, v_ref, g_ref, lr_ref, w2_ref, gb_ref, wstk_ref, lvl_ref, onw_ref,
                o_ref, state_ref, b2_ref, *, seq):
    c = GLA_CHUNK
    dk = GLA_HEAD_K
    n_chunks = seq // c
    w2 = w2_ref[...]
    gb = gb_ref[...]
    onw = onw_ref[...]
    eye = (lax.broadcasted_iota(jnp.int32, (dk, dk), 0)
           == lax.broadcasted_iota(jnp.int32, (dk, dk), 1))
    nt = (((1,), (1,)), ((), ()))
    tn = (((0,), (0,)), ((), ()))
    norm_eps = NORM_EPS * dk

    def log_decay(r0):
        z = jnp.dot(lr_ref[pl.ds(r0, c), :], w2, preferred_element_type=F32) + gb
        la = (jnp.minimum(z, 0.0) - jnp.log1p(jnp.exp(-jnp.abs(z)))) * (1.0 / GLA_GATE_TAU)
        la_hi = la.astype(BF16)
        la_lo = (la - la_hi.astype(F32)).astype(BF16)
        return jnp.concatenate([la_hi, la_lo], axis=1)

    def finish(r0, o, kd, vb, e_last, state):
        g = g_ref[pl.ds(r0, c), :].astype(F32)
        y = o * lax.rsqrt(jnp.mean(o * o, axis=-1, keepdims=True) + norm_eps) * onw
        y = y * (g * jax.nn.sigmoid(g))
        o_ref[pl.ds(r0, c), :] = y.astype(o_ref.dtype)
        e_col = jnp.sum(jnp.where(eye, e_last, 0.0), axis=1, keepdims=True)
        state_ref[...] = state * e_col + lax.dot_general(kd, vb, tn, preferred_element_type=F32)

    def prepass(ci, tot):
        r0 = pl.multiple_of(ci * c, c)
        cs = jnp.dot(wstk_ref[0:c, :], log_decay(r0), preferred_element_type=F32)
        b2 = (cs[:, :dk] + cs[:, dk:]) * LOG2E
        b2_ref[pl.ds(r0, c), :] = b2
        return jnp.maximum(tot, -b2[c - 1:c, :])

    tot = lax.fori_loop(0, n_chunks, prepass, jnp.zeros((1, dk), F32), unroll=4)
    bounded = jnp.max(tot, axis=-1, keepdims=True)[0, 0] <= GLA_FAST_MAX_DECAY
    state_ref[...] = jnp.zeros_like(state_ref)

    def fast_chunk(ci, carry):
        r0 = pl.multiple_of(ci * c, c)
        qf = q_ref[pl.ds(r0, c), :].astype(F32)
        kf = k_ref[pl.ds(r0, c), :].astype(F32)
        vb = v_ref[pl.ds(r0, c), :]
        b2 = b2_ref[pl.ds(r0, c), :]
        eq = jnp.exp2(b2)
        qt = (qf * eq).astype(BF16)
        kt = kf * jnp.exp2(-b2)
        state = state_ref[...]
        causal = (lax.broadcasted_iota(jnp.int32, (c, c), 0)
                  >= lax.broadcasted_iota(jnp.int32, (c, c), 1))
        p = lax.dot_general(qt, kt.astype(BF16), nt, preferred_element_type=F32)
        o = jnp.dot(qt, state.astype(BF16), preferred_element_type=F32)
        o = o + jnp.dot(jnp.where(causal, p, 0.0).astype(BF16), vb, preferred_element_type=F32)
        e_last = eq[c - 1:c, :]
        finish(r0, o, (kt * e_last).astype(BF16), vb, e_last, state)
        return carry

    def level_chunk(ci, carry):
        r0 = pl.multiple_of(ci * c, c)
        lvl = lvl_ref[...]
        row = lax.broadcasted_iota(jnp.int32, (c, 1), 0)
        qf = q_ref[pl.ds(r0, c), :].astype(F32)
        kf = k_ref[pl.ds(r0, c), :].astype(F32)
        kb = k_ref[pl.ds(r0, c), :]
        vb = v_ref[pl.ds(r0, c), :]
        sums = jnp.dot(wstk_ref[...], log_decay(r0), preferred_element_type=F32)
        sums = sums[:, :dk] + sums[:, dk:]
        b = sums[0:c]
        b_rev = sums[c:2 * c]
        state = state_ref[...]
        o = jnp.dot((qf * jnp.exp(b)).astype(BF16), state.astype(BF16),
                    preferred_element_type=F32)
        attn = jnp.where(lvl == -1,
                         lax.dot_general(qf.astype(BF16), kb, nt, preferred_element_type=F32), 0.0)
        for l in range(GLA_LEVELS):
            e = jnp.exp(sums[(l + 2) * c:(l + 3) * c])
            upper = ((row >> l) & 1) == 1
            qt = jnp.where(upper, qf * e, 0.0).astype(BF16)
            kt = jnp.where(upper, 0.0, kf * e).astype(BF16)
            p = lax.dot_general(qt, kt, nt, preferred_element_type=F32)
            attn = jnp.where(lvl == l, p, attn)
        o = o + jnp.dot(attn.astype(BF16), vb, preferred_element_type=F32)
        finish(r0, o, (kf * jnp.exp(b_rev)).astype(BF16), vb, jnp.exp(b[c - 1:c, :]), state)
        return carry

    @pl.when(bounded)
    def _():
        lax.fori_loop(0, n_chunks, fast_chunk, 0, unroll=4)

    @pl.when(jnp.logical_not(bounded))
    def _():
        lax.fori_loop(0, n_chunks, level_chunk, 0)


def _gla(proj, lr, w2p, gate_b, wstack, level, out_norm_w, batch, seq):
    m = proj.shape[0]
    kq = GLA_HEAD_K
    kv = GLA_HEAD_V
    rows = wstack.shape[0]
    return pl.pallas_call(
        functools.partial(_gla_kernel, seq=seq),
        grid=(batch, GLA_HEADS),
        in_specs=[
            pl.BlockSpec((seq, kq), lambda b, h: (b, COL_GQ // kq + h)),
            pl.BlockSpec((seq, kq), lambda b, h: (b, COL_GK // kq + h)),
            pl.BlockSpec((seq, kv), lambda b, h: (b, COL_GV // kv + h)),
            pl.BlockSpec((seq, kv), lambda b, h: (b, COL_GG // kv + h)),
            pl.BlockSpec((seq, V7X_LANES), lambda b, h: (b, 0)),
            pl.BlockSpec((V7X_LANES, kq), lambda b, h: (0, h)),
            pl.BlockSpec((1, kq), lambda b, h: (0, h)),
            pl.BlockSpec((rows, GLA_CHUNK), lambda b, h: (0, 0)),
            pl.BlockSpec((GLA_CHUNK, GLA_CHUNK), lambda b, h: (0, 0)),
            pl.BlockSpec((1, kv), lambda b, h: (0, 0)),
        ],
        out_specs=pl.BlockSpec((seq, kv), lambda b, h: (b, h)),
        out_shape=jax.ShapeDtypeStruct((m, GLA_WIDTH), BF16),
        scratch_shapes=[pltpu.VMEM((kq, kv), F32), pltpu.VMEM((seq, kq), F32)],
        compiler_params=_cparams(("parallel", "parallel")),
        name="gla",
    )(proj, proj, proj, proj, lr, w2p, gate_b, wstack, level, out_norm_w)


def _diff_kernel(q_ref, k_ref, v_ref, cos_ref, sin_ref, qnw_ref, knw_ref, qnr_ref, knr_ref,
                 lq1_ref, lk1_ref, lq2_ref, lk2_ref, sw_ref, o_ref, qs_ref, ks_ref, tab_ref,
                 *, seq, lam_init):
    dh = DIFF_HEAD_DIM
    tq = Q_TILE
    lam = (jnp.exp(jnp.sum(lq1_ref[...] * lk1_ref[...], axis=-1, keepdims=True))
           - jnp.exp(jnp.sum(lq2_ref[...] * lk2_ref[...], axis=-1, keepdims=True))
           + lam_init)
    qscale = (dh ** -0.5) * LOG2E

    @pl.when(pl.program_id(1) == 0)
    def _():
        cosf = cos_ref[...]
        sinf = sin_ref[...]
        tab_ref[0] = cosf * (qnw_ref[...] * qscale)
        tab_ref[1] = sinf * (qnr_ref[...] * qscale)
        tab_ref[2] = cosf * knw_ref[...]
        tab_ref[3] = sinf * knr_ref[...]

    perm = (lax.broadcasted_iota(jnp.int32, (dh, dh), 0)
            == ((lax.broadcasted_iota(jnp.int32, (dh, dh), 1) + dh // 2) % dh)).astype(BF16)

    def prep(src_ref, tab, dst_ref):
        for comp in range(2):
            tb = src_ref[:, comp * dh:(comp + 1) * dh]
            t = tb.astype(F32)
            r = lax.rsqrt(jnp.mean(t * t, axis=-1, keepdims=True) + NORM_EPS)
            tr = jnp.dot(tb, perm, preferred_element_type=F32)
            dst_ref[comp] = (r * (t * tab_ref[tab] + tr * tab_ref[tab + 1])).astype(BF16)

    prep(q_ref, 0, qs_ref)
    prep(k_ref, 2, ks_ref)

    nt = (((1,), (1,)), ((), ()))
    tri = (lax.broadcasted_iota(jnp.int32, (tq, tq), 0)
           >= lax.broadcasted_iota(jnp.int32, (tq, tq), 1))
    sw = sw_ref[...]

    def attend(subtract_max):
        for t in range(seq // tq):
            r0 = t * tq
            outs = []
            for comp in range(2):
                q = qs_ref[comp, r0:r0 + tq, :]
                s = lax.dot_general(q, ks_ref[comp, r0:r0 + tq, :], nt,
                                    preferred_element_type=F32)
                s = jnp.where(tri, s, -jnp.inf)
                if t > 0:
                    s_off = lax.dot_general(q, ks_ref[comp, 0:r0, :], nt,
                                            preferred_element_type=F32)
                    s = jnp.concatenate([s_off, s], axis=1)
                if subtract_max:
                    s = s - jnp.max(s, axis=-1, keepdims=True)
                p = jnp.exp2(s)
                l = jnp.sum(p, axis=-1, keepdims=True)
                pv = jnp.dot(p.astype(BF16), v_ref[0:r0 + tq, :], preferred_element_type=F32)
                outs.append(pv * (1.0 / l))
            o = outs[0] - lam * outs[1]
            o = o * lax.rsqrt(jnp.mean(o * o, axis=-1, keepdims=True) + SUBLN_EPS) * sw
            o_ref[r0:r0 + tq, :] = (o * (1.0 - lam_init)).astype(o_ref.dtype)

    bound = (dh * qscale) * (jnp.max(jnp.abs(qnw_ref[...]), axis=-1, keepdims=True)
                             * jnp.max(jnp.abs(knw_ref[...]), axis=-1, keepdims=True))
    small = bound[0, 0] <= SCORE_BOUND_NO_SHIFT

    @pl.when(small)
    def _():
        attend(False)

    @pl.when(jnp.logical_not(small))
    def _():
        attend(True)


def _diff_attn(proj, cosf, sinf, qnw, knw, qnr, knr, lq1, lk1, lq2, lk2, subln_w, batch, seq,
               lam_init):
    m = proj.shape[0]
    hw = 2 * DIFF_HEAD_DIM
    vec = pl.BlockSpec((1, DIFF_HEAD_DIM), lambda b, h: (0, 0))
    return pl.pallas_call(
        functools.partial(_diff_kernel, seq=seq, lam_init=lam_init),
        grid=(batch, DIFF_HEADS),
        in_specs=[
            pl.BlockSpec((seq, hw), lambda b, h: (b, COL_DQ // hw + h)),
            pl.BlockSpec((seq, hw), lambda b, h: (b, COL_DK // hw + h)),
            pl.BlockSpec((seq, hw), lambda b, h: (b, COL_DV // hw + h)),
            pl.BlockSpec((seq, DIFF_HEAD_DIM), lambda b, h: (b, 0)),
            pl.BlockSpec((seq, DIFF_HEAD_DIM), lambda b, h: (b, 0)),
            vec, vec, vec, vec, vec, vec, vec, vec,
            pl.BlockSpec((1, hw), lambda b, h: (0, 0)),
        ],
        out_specs=pl.BlockSpec((seq, hw), lambda b, h: (b, h)),
        out_shape=jax.ShapeDtypeStruct((m, DIFF_WIDTH), BF16),
        scratch_shapes=[pltpu.VMEM((2, seq, DIFF_HEAD_DIM), BF16),
                        pltpu.VMEM((2, seq, DIFF_HEAD_DIM), BF16),
                        pltpu.VMEM((4, seq, DIFF_HEAD_DIM), F32)],
        compiler_params=_cparams(("parallel", "arbitrary")),
        name="diff_attn",
    )(proj, proj, proj, cosf, sinf, qnw, knw, qnr, knr, lq1, lk1, lq2, lk2, subln_w)


def _out_proj_kernel(x_ref, yg_ref, yd_ref, wg_ref, wd_ref, o_ref):
    acc = jnp.dot(yg_ref[...], wg_ref[...], preferred_element_type=F32)
    acc = acc + jnp.dot(yd_ref[...], wd_ref[...], preferred_element_type=F32)
    o_ref[...] = x_ref[...] + acc


def _out_proj(x2, yg, yd, wg, wd, tm):
    m, d = x2.shape
    return pl.pallas_call(
        _out_proj_kernel,
        grid=(m // tm,),
        in_specs=[
            pl.BlockSpec((tm, d), lambda i: (i, 0)),
            pl.BlockSpec((tm, GLA_WIDTH), lambda i: (i, 0)),
            pl.BlockSpec((tm, DIFF_WIDTH), lambda i: (i, 0)),
            pl.BlockSpec((GLA_WIDTH, d), lambda i: (0, 0)),
            pl.BlockSpec((DIFF_WIDTH, d), lambda i: (0, 0)),
        ],
        out_specs=pl.BlockSpec((tm, d), lambda i: (i, 0)),
        out_shape=jax.ShapeDtypeStruct((m, d), F32),
        compiler_params=_cparams(("parallel",)),
        name="out_proj",
    )(x2, yg, yd, wg, wd)


def _mlp_kernel(x_ref, nw_ref, wu_ref, wd_ref, o_ref, n_ref):
    @pl.when(pl.program_id(1) == 0)
    def _():
        x = x_ref[...]
        ms = jnp.mean(x * x, axis=-1, keepdims=True)
        n_ref[...] = (x * lax.rsqrt(ms + NORM_EPS) * nw_ref[...]).astype(BF16)
        o_ref[...] = x

    h = jnp.dot(n_ref[...], wu_ref[...], preferred_element_type=F32)
    h = jnp.square(jnp.maximum(h, 0.0)).astype(BF16)
    o_ref[...] += jnp.dot(h, wd_ref[...], preferred_element_type=F32)


def _mlp(x2, norm_w, w_up, w_down, tm, tf):
    m, d = x2.shape
    f = w_up.shape[1]
    return pl.pallas_call(
        _mlp_kernel,
        grid=(m // tm, f // tf),
        in_specs=[
            pl.BlockSpec((tm, d), lambda i, j: (i, 0)),
            pl.BlockSpec((1, d), lambda i, j: (0, 0)),
            pl.BlockSpec((d, tf), lambda i, j: (0, j)),
            pl.BlockSpec((tf, d), lambda i, j: (j, 0)),
        ],
        out_specs=pl.BlockSpec((tm, d), lambda i, j: (i, 0)),
        out_shape=jax.ShapeDtypeStruct((m, d), F32),
        scratch_shapes=[pltpu.VMEM((tm, d), BF16)],
        compiler_params=_cparams(("parallel", "arbitrary")),
        name="mlp",
    )(x2, norm_w, w_up, w_down)


def _lambda_init(layer_idx):
    return 0.8 - 0.6 * math.exp(-0.3 * layer_idx)


def kernel(x, positions, attn_norm_w, w_in, gla_gate_w2, gla_gate_b, gla_out_norm_w,
           diff_q_norm_w, diff_k_norm_w, diff_lambda_q1, diff_lambda_k1, diff_lambda_q2,
           diff_lambda_k2, diff_subln_w, w_out, mlp_norm_w, w_up, w_down):
    batch, seq, d = x.shape
    depth = w_in.shape[0]
    m = batch * seq
    assert d == GLA_WIDTH + DIFF_WIDTH and seq % GLA_CHUNK == 0 and seq % Q_TILE == 0
    tm_proj = min(1024, m)
    tm_out = min(512, m)
    tm_mlp = min(512, m)

    half = DIFF_HEAD_DIM // 2
    inv_freq = ROPE_THETA ** (-jnp.arange(0, DIFF_HEAD_DIM, 2, dtype=F32) / DIFF_HEAD_DIM)
    ang = positions.astype(F32)[..., None] * inv_freq
    cos, sin = jnp.cos(ang), jnp.sin(ang)
    cosf = jnp.concatenate([cos, cos], axis=-1).reshape(m, 2 * half)
    sinf = jnp.concatenate([-sin, sin], axis=-1).reshape(m, 2 * half)

    wstack_np, level_np = _gla_constants()
    wstack = jnp.asarray(wstack_np, dtype=BF16)
    level = jnp.asarray(level_np)

    sizes = (GLA_KEY_WIDTH, GLA_KEY_WIDTH, GLA_WIDTH, GLA_WIDTH, GLA_GATE_RANK,
             DIFF_WIDTH, DIFF_WIDTH, DIFF_WIDTH)
    offs = np.concatenate([[0], np.cumsum(sizes)])
    lr0, lr1 = int(offs[4]), int(offs[5])

    x2 = x.reshape(m, d)
    for layer in range(depth):
        wl = w_in[layer]
        w_main = jnp.concatenate([wl[:, :lr0], wl[:, lr1:]], axis=1).astype(BF16)
        w_lr = jnp.pad(wl[:, lr0:lr1], ((0, 0), (0, V7X_LANES - GLA_GATE_RANK))).astype(BF16)
        w2p = jnp.pad(gla_gate_w2[layer], ((0, V7X_LANES - GLA_GATE_RANK), (0, 0))).astype(BF16)

        proj, lr = _in_proj(x2, attn_norm_w[layer][None, :], w_main, w_lr, tm_proj, 1024)
        y_gla = _gla(proj, lr, w2p, gla_gate_b[layer][None, :], wstack, level,
                     gla_out_norm_w[layer][None, :], batch, seq)
        y_diff = _diff_attn(proj, cosf, sinf,
                            diff_q_norm_w[layer][None, :], diff_k_norm_w[layer][None, :],
                            jnp.roll(diff_q_norm_w[layer], DIFF_HEAD_DIM // 2)[None, :],
                            jnp.roll(diff_k_norm_w[layer], DIFF_HEAD_DIM // 2)[None, :],
                            diff_lambda_q1[layer][None, :], diff_lambda_k1[layer][None, :],
                            diff_lambda_q2[layer][None, :], diff_lambda_k2[layer][None, :],
                            diff_subln_w[layer][None, :], batch, seq, _lambda_init(layer))
        wo = w_out[layer].astype(BF16)
        x2 = _out_proj(x2, y_gla, y_diff, wo[:GLA_WIDTH], wo[GLA_WIDTH:], tm_out)
        x2 = _mlp(x2, mlp_norm_w[layer][None, :], w_up[layer].astype(BF16),
                  w_down[layer].astype(BF16), tm_mlp, 1024)
    return x2.reshape(batch, seq, d)
```

```python
import functools
import math

import numpy as np
import jax
import jax.numpy as jnp
from jax import lax
from jax.experimental import pallas as pl
from jax.experimental.pallas import tpu as pltpu

F32 = jnp.float32
BF16 = jnp.bfloat16

GLA_HEADS = 4
GLA_HEAD_K = 128
GLA_HEAD_V = 256
GLA_KEY_WIDTH = GLA_HEADS * GLA_HEAD_K
GLA_WIDTH = GLA_HEADS * GLA_HEAD_V
GLA_GATE_RANK = 16
GLA_GATE_TAU = 16.0
DIFF_HEADS = 4
DIFF_HEAD_DIM = 128
DIFF_WIDTH = DIFF_HEADS * 2 * DIFF_HEAD_DIM
ROPE_THETA = 10000.0
NORM_EPS = 1e-6
SUBLN_EPS = 1e-5
LOG2E = math.log2(math.e)

V7X_LANES = 128
V7X_VMEM_BYTES = 64 * 1024 * 1024
VMEM_LIMIT_BYTES = 56 * 1024 * 1024

COL_GQ, COL_GK, COL_GV, COL_GG = 0, 512, 1024, 2048
COL_DQ, COL_DK, COL_DV = 3072, 4096, 5120
MAIN_COLS = 6144

GLA_CHUNK = 256
GLA_LEVELS = int(math.log2(GLA_CHUNK))
GLA_FAST_MAX_DECAY = 64.0
Q_TILE = 256
SCORE_BOUND_NO_SHIFT = 60.0


def _cparams(sem):
    return pltpu.CompilerParams(dimension_semantics=sem, vmem_limit_bytes=VMEM_LIMIT_BYTES)


def _in_proj_kernel(x_ref, nw_ref, w_ref, wlr_ref, o_ref, olr_ref, n_ref):
    @pl.when(pl.program_id(1) == 0)
    def _():
        x = x_ref[...]
        ms = jnp.mean(x * x, axis=-1, keepdims=True)
        n = (x * lax.rsqrt(ms + NORM_EPS) * nw_ref[...]).astype(BF16)
        n_ref[...] = n
        olr_ref[...] = jnp.dot(n, wlr_ref[...], preferred_element_type=F32).astype(BF16)

    o_ref[...] = jnp.dot(n_ref[...], w_ref[...], preferred_element_type=F32).astype(BF16)


def _in_proj(x2, norm_w, w_main, w_lr, layer, tm, tn):
    m, d = x2.shape
    return pl.pallas_call(
        _in_proj_kernel,
        grid=(m // tm, MAIN_COLS // tn),
        in_specs=[
            pl.BlockSpec((tm, d), lambda i, j: (i, 0)),
            pl.BlockSpec((1, d), lambda i, j: (0, 0)),
            pl.BlockSpec((None, d, tn), lambda i, j: (layer, 0, j)),
            pl.BlockSpec((None, d, V7X_LANES), lambda i, j: (layer, 0, 0)),
        ],
        out_specs=[
            pl.BlockSpec((tm, tn), lambda i, j: (i, j)),
            pl.BlockSpec((tm, V7X_LANES), lambda i, j: (i, 0)),
        ],
        out_shape=[
            jax.ShapeDtypeStruct((m, MAIN_COLS), BF16),
            jax.ShapeDtypeStruct((m, V7X_LANES), BF16),
        ],
        scratch_shapes=[pltpu.VMEM((tm, d), BF16)],
        compiler_params=_cparams(("parallel", "arbitrary")),
        name="in_proj",
    )(x2, norm_w, w_main, w_lr)


def _gla_constants():
    c = GLA_CHUNK
    i = np.arange(c)[:, None]
    t = np.arange(c)[None, :]
    mats = [(t <= i), (t > i)]
    for lvl in range(GLA_LEVELS):
        s = 1 << lvl
        m = (i // (2 * s)) * (2 * s) + s - 1
        mats.append((t > np.minimum(i, m)) & (t <= np.maximum(i, m)))
    wstack = np.concatenate(mats, axis=0).astype(np.float32)
    x = i ^ t
    level = np.where(i > t, np.floor(np.log2(np.maximum(x, 1))), np.where(i == t, -1, -2))
    return wstack, level.astype(np.int32)


def _gla_kernel(q_ref, k_ref, v_ref, g_ref, lr_ref, w2_ref, gb_ref, wstk_ref, lvl_ref, onw_ref,
                o_ref, state_ref, b2_ref, *, seq):
    c = GLA_CHUNK
    dk = GLA_HEAD_K
    n_chunks = seq // c
    w2 = w2_ref[...]
    gb = gb_ref[...]
    onw = onw_ref[...]
    eye = (lax.broadcasted_iota(jnp.int32, (dk, dk), 0)
           == lax.broadcasted_iota(jnp.int32, (dk, dk), 1))
    nt = (((1,), (1,)), ((), ()))
    tn = (((0,), (0,)), ((), ()))
    norm_eps = NORM_EPS * dk

    def log_decay(r0):
        z = jnp.dot(lr_ref[pl.ds(r0, c), :], w2, preferred_element_type=F32) + gb
        la = (jnp.minimum(z, 0.0) - jnp.log1p(jnp.exp(-jnp.abs(z)))) * (1.0 / GLA_GATE_TAU)
        la_hi = la.astype(BF16)
        la_lo = (la - la_hi.astype(F32)).astype(BF16)
        return jnp.concatenate([la_hi, la_lo], axis=1)

    def finish(r0, o, kd, vb, e_last, state):
        g = g_ref[pl.ds(r0, c), :].astype(F32)
        y = o * lax.rsqrt(jnp.mean(o * o, axis=-1, keepdims=True) + norm_eps) * onw
        y = y * (g * jax.nn.sigmoid(g))
        o_ref[pl.ds(r0, c), :] = y.astype(o_ref.dtype)
        e_col = jnp.sum(jnp.where(eye, e_last, 0.0), axis=1, keepdims=True)
        state_ref[...] = state * e_col + lax.dot_general(kd, vb, tn, preferred_element_type=F32)

    def prepass(ci, tot):
        r0 = pl.multiple_of(ci * c, c)
        cs = jnp.dot(wstk_ref[0:c, :], log_decay(r0), preferred_element_type=F32)
        b2 = (cs[:, :dk] + cs[:, dk:]) * LOG2E
        b2_ref[pl.ds(r0, c), :] = b2
        return jnp.maximum(tot, -b2[c - 1:c, :])

    tot = lax.fori_loop(0, n_chunks, prepass, jnp.zeros((1, dk), F32), unroll=4)
    bounded = jnp.max(tot, axis=-1, keepdims=True)[0, 0] <= GLA_FAST_MAX_DECAY
    state_ref[...] = jnp.zeros_like(state_ref)

    def fast_chunk(ci, carry):
        r0 = pl.multiple_of(ci * c, c)
        qf = q_ref[pl.ds(r0, c), :].astype(F32)
        kf = k_ref[pl.ds(r0, c), :].astype(F32)
        vb = v_ref[pl.ds(r0, c), :]
        b2 = b2_ref[pl.ds(r0, c), :]
        eq = jnp.exp2(b2)
        qt = (qf * eq).astype(BF16)
        kt = kf * jnp.exp2(-b2)
        state = state_ref[...]
        causal = (lax.broadcasted_iota(jnp.int32, (c, c), 0)
                  >= lax.broadcasted_iota(jnp.int32, (c, c), 1))
        p = lax.dot_general(qt, kt.astype(BF16), nt, preferred_element_type=F32)
        o = jnp.dot(qt, state.astype(BF16), preferred_element_type=F32)
        o = o + jnp.dot(jnp.where(causal, p, 0.0).astype(BF16), vb, preferred_element_type=F32)
        e_last = eq[c - 1:c, :]
        finish(r0, o, (kt * e_last).astype(BF16), vb, e_last, state)
        return carry

    def level_chunk(ci, carry):
        r0 = pl.multiple_of(ci * c, c)
        lvl = lvl_ref[...]
        row = lax.broadcasted_iota(jnp.int32, (c, 1), 0)
        qf = q_ref[pl.ds(r0, c), :].astype(F32)
        kf = k_ref[pl.ds(r0, c), :].astype(F32)
        kb = k_ref[pl.ds(r0, c), :]
        vb = v_ref[pl.ds(r0, c), :]
        sums = jnp.dot(wstk_ref[...], log_decay(r0), preferred_element_type=F32)
        sums = sums[:, :dk] + sums[:, dk:]
        b = sums[0:c]
        b_rev = sums[c:2 * c]
        state = state_ref[...]
        o = jnp.dot((qf * jnp.exp(b)).astype(BF16), state.astype(BF16),
                    preferred_element_type=F32)
        attn = jnp.where(lvl == -1,
                         lax.dot_general(qf.astype(BF16), kb, nt, preferred_element_type=F32), 0.0)
        for l in range(GLA_LEVELS):
            e = jnp.exp(sums[(l + 2) * c:(l + 3) * c])
            upper = ((row >> l) & 1) == 1
            qt = jnp.where(upper, qf * e, 0.0).astype(BF16)
            kt = jnp.where(upper, 0.0, kf * e).astype(BF16)
            p = lax.dot_general(qt, kt, nt, preferred_element_type=F32)
            attn = jnp.where(lvl == l, p, attn)
        o = o + jnp.dot(attn.astype(BF16), vb, preferred_element_type=F32)
        finish(r0, o, (kf * jnp.exp(b_rev)).astype(BF16), vb, jnp.exp(b[c - 1:c, :]), state)
        return carry

    @pl.when(bounded)
    def _():
        lax.fori_loop(0, n_chunks, fast_chunk, 0, unroll=4)

    @pl.when(jnp.logical_not(bounded))
    def _():
        lax.fori_loop(0, n_chunks, level_chunk, 0)


def _gla(proj, lr, w2p, gate_b, wstack, level, out_norm_w, batch, seq):
    m = proj.shape[0]
    kq = GLA_HEAD_K
    kv = GLA_HEAD_V
    rows = wstack.shape[0]
    return pl.pallas_call(
        functools.partial(_gla_kernel, seq=seq),
        grid=(batch, GLA_HEADS),
        in_specs=[
            pl.BlockSpec((seq, kq), lambda b, h: (b, COL_GQ // kq + h)),
            pl.BlockSpec((seq, kq), lambda b, h: (b, COL_GK // kq + h)),
            pl.BlockSpec((seq, kv), lambda b, h: (b, COL_GV // kv + h)),
            pl.BlockSpec((seq, kv), lambda b, h: (b, COL_GG // kv + h)),
            pl.BlockSpec((seq, V7X_LANES), lambda b, h: (b, 0)),
            pl.BlockSpec((V7X_LANES, kq), lambda b, h: (0, h)),
            pl.BlockSpec((1, kq), lambda b, h: (0, h)),
            pl.BlockSpec((rows, GLA_CHUNK), lambda b, h: (0, 0)),
            pl.BlockSpec((GLA_CHUNK, GLA_CHUNK), lambda b, h: (0, 0)),
            pl.BlockSpec((1, kv), lambda b, h: (0, 0)),
        ],
        out_specs=pl.BlockSpec((seq, kv), lambda b, h: (b, h)),
        out_shape=jax.ShapeDtypeStruct((m, GLA_WIDTH), BF16),
        scratch_shapes=[pltpu.VMEM((kq, kv), F32), pltpu.VMEM((seq, kq), F32)],
        compiler_params=_cparams(("parallel", "parallel")),
        name="gla",
    )(proj, proj, proj, proj, lr, w2p, gate_b, wstack, level, out_norm_w)


def _diff_kernel(q_ref, k_ref, v_ref, cos_ref, sin_ref, qnw_ref, knw_ref, qnr_ref, knr_ref,
                 lq1_ref, lk1_ref, lq2_ref, lk2_ref, sw_ref, o_ref, qs_ref, ks_ref, tab_ref,
                 *, seq, lam_init):
    dh = DIFF_HEAD_DIM
    tq = Q_TILE
    lam = (jnp.exp(jnp.sum(lq1_ref[...] * lk1_ref[...], axis=-1, keepdims=True))
           - jnp.exp(jnp.sum(lq2_ref[...] * lk2_ref[...], axis=-1, keepdims=True))
           + lam_init)
    qscale = (dh ** -0.5) * LOG2E

    @pl.when(pl.program_id(1) == 0)
    def _():
        cosf = cos_ref[...]
        sinf = sin_ref[...]
        tab_ref[0] = cosf * (qnw_ref[...] * qscale)
        tab_ref[1] = sinf * (qnr_ref[...] * qscale)
        tab_ref[2] = cosf * knw_ref[...]
        tab_ref[3] = sinf * knr_ref[...]

    perm = (lax.broadcasted_iota(jnp.int32, (dh, dh), 0)
            == ((lax.broadcasted_iota(jnp.int32, (dh, dh), 1) + dh // 2) % dh)).astype(BF16)

    def prep(src_ref, tab, comp, r0, rows):
        tb = src_ref[r0:r0 + rows, comp * dh:(comp + 1) * dh]
        t = tb.astype(F32)
        r = lax.rsqrt(jnp.mean(t * t, axis=-1, keepdims=True) + NORM_EPS)
        tr = jnp.dot(tb, perm, preferred_element_type=F32)
        return (r * (t * tab_ref[tab, r0:r0 + rows, :]
                     + tr * tab_ref[tab + 1, r0:r0 + rows, :])).astype(BF16)

    for comp in range(2):
        qs_ref[comp] = prep(q_ref, 0, comp, 0, seq)
        ks_ref[comp] = prep(k_ref, 2, comp, 0, seq)

    nt = (((1,), (1,)), ((), ()))
    tri = (lax.broadcasted_iota(jnp.int32, (tq, tq), 0)
           >= lax.broadcasted_iota(jnp.int32, (tq, tq), 1))
    sw = sw_ref[...] * (1.0 - lam_init)

    def attend(subtract_max):
        for t in reversed(range(seq // tq)):
            r0 = t * tq
            ps, ls = [], []
            for comp in range(2):
                q = qs_ref[comp, r0:r0 + tq, :]
                s = lax.dot_general(q, ks_ref[comp, r0:r0 + tq, :], nt,
                                    preferred_element_type=F32)
                s = jnp.where(tri, s, -jnp.inf)
                if t > 0:
                    s_off = lax.dot_general(q, ks_ref[comp, 0:r0, :], nt,
                                            preferred_element_type=F32)
                    s = jnp.concatenate([s_off, s], axis=1)
                if subtract_max:
                    s = s - jnp.max(s, axis=-1, keepdims=True)
                p = jnp.exp2(s)
                ls.append(jnp.sum(p, axis=-1, keepdims=True))
                ps.append(p.astype(BF16))
            ratio = (lam * ls[0] / ls[1]).astype(BF16)
            w = ps[0] - ratio * ps[1]
            o = jnp.dot(w, v_ref[0:r0 + tq, :], preferred_element_type=F32) * (1.0 / ls[0])
            o = o * lax.rsqrt(jnp.mean(o * o, axis=-1, keepdims=True) + SUBLN_EPS) * sw
            o_ref[r0:r0 + tq, :] = o.astype(o_ref.dtype)

    bound = (dh * qscale) * (jnp.max(jnp.abs(qnw_ref[...]), axis=-1, keepdims=True)
                             * jnp.max(jnp.abs(knw_ref[...]), axis=-1, keepdims=True))
    small = bound[0, 0] <= SCORE_BOUND_NO_SHIFT

    @pl.when(small)
    def _():
        attend(False)

    @pl.when(jnp.logical_not(small))
    def _():
        attend(True)


def _diff_attn(proj, cosf, sinf, qnw, knw, qnr, knr, lq1, lk1, lq2, lk2, subln_w, batch, seq,
               lam_init):
    m = proj.shape[0]
    hw = 2 * DIFF_HEAD_DIM
    vec = pl.BlockSpec((1, DIFF_HEAD_DIM), lambda b, h: (0, 0))
    return pl.pallas_call(
        functools.partial(_diff_kernel, seq=seq, lam_init=lam_init),
        grid=(batch, DIFF_HEADS),
        in_specs=[
            pl.BlockSpec((seq, hw), lambda b, h: (b, COL_DQ // hw + h)),
            pl.BlockSpec((seq, hw), lambda b, h: (b, COL_DK // hw + h)),
            pl.BlockSpec((seq, hw), lambda b, h: (b, COL_DV // hw + h)),
            pl.BlockSpec((seq, DIFF_HEAD_DIM), lambda b, h: (b, 0)),
            pl.BlockSpec((seq, DIFF_HEAD_DIM), lambda b, h: (b, 0)),
            vec, vec, vec, vec, vec, vec, vec, vec,
            pl.BlockSpec((1, hw), lambda b, h: (0, 0)),
        ],
        out_specs=pl.BlockSpec((seq, hw), lambda b, h: (b, h)),
        out_shape=jax.ShapeDtypeStruct((m, DIFF_WIDTH), BF16),
        scratch_shapes=[pltpu.VMEM((2, seq, DIFF_HEAD_DIM), BF16),
                        pltpu.VMEM((2, seq, DIFF_HEAD_DIM), BF16),
                        pltpu.VMEM((4, seq, DIFF_HEAD_DIM), F32)],
        compiler_params=_cparams(("parallel", "arbitrary")),
        name="diff_attn",
    )(proj, proj, proj, cosf, sinf, qnw, knw, qnr, knr, lq1, lk1, lq2, lk2, subln_w)


def _out_proj_kernel(x_ref, yg_ref, yd_ref, wg_ref, wd_ref, o_ref):
    acc = jnp.dot(yg_ref[...], wg_ref[...], preferred_element_type=F32)
    acc = acc + jnp.dot(yd_ref[...], wd_ref[...], preferred_element_type=F32)
    o_ref[...] = x_ref[...] + acc


def _out_proj(x2, yg, yd, w_out, layer, tm):
    m, d = x2.shape
    return pl.pallas_call(
        _out_proj_kernel,
        grid=(m // tm,),
        in_specs=[
            pl.BlockSpec((tm, d), lambda i: (i, 0)),
            pl.BlockSpec((tm, GLA_WIDTH), lambda i: (i, 0)),
            pl.BlockSpec((tm, DIFF_WIDTH), lambda i: (i, 0)),
            pl.BlockSpec((None, GLA_WIDTH, d), lambda i: (layer, 0, 0)),
            pl.BlockSpec((None, DIFF_WIDTH, d), lambda i: (layer, GLA_WIDTH // DIFF_WIDTH, 0)),
        ],
        out_specs=pl.BlockSpec((tm, d), lambda i: (i, 0)),
        out_shape=jax.ShapeDtypeStruct((m, d), F32),
        compiler_params=_cparams(("parallel",)),
        name="out_proj",
    )(x2, yg, yd, w_out, w_out)


def _mlp_kernel(x_ref, nw_ref, wu_ref, wd_ref, o_ref, n_ref):
    @pl.when(pl.program_id(1) == 0)
    def _():
        x = x_ref[...]
        ms = jnp.mean(x * x, axis=-1, keepdims=True)
        n_ref[...] = (x * lax.rsqrt(ms + NORM_EPS) * nw_ref[...]).astype(BF16)
        o_ref[...] = x

    h = jnp.dot(n_ref[...], wu_ref[...], preferred_element_type=F32)
    h = jnp.square(jnp.maximum(h, 0.0)).astype(BF16)
    o_ref[...] += jnp.dot(h, wd_ref[...], preferred_element_type=F32)


def _mlp(x2, norm_w, w_up, w_down, layer, tm, tf):
    m, d = x2.shape
    f = w_up.shape[2]
    return pl.pallas_call(
        _mlp_kernel,
        grid=(m // tm, f // tf),
        in_specs=[
            pl.BlockSpec((tm, d), lambda i, j: (i, 0)),
            pl.BlockSpec((1, d), lambda i, j: (0, 0)),
            pl.BlockSpec((None, d, tf), lambda i, j: (layer, 0, j)),
            pl.BlockSpec((None, tf, d), lambda i, j: (layer, j, 0)),
        ],
        out_specs=pl.BlockSpec((tm, d), lambda i, j: (i, 0)),
        out_shape=jax.ShapeDtypeStruct((m, d), F32),
        scratch_shapes=[pltpu.VMEM((tm, d), BF16)],
        compiler_params=_cparams(("parallel", "arbitrary")),
        name="mlp",
    )(x2, norm_w, w_up, w_down)


def _lambda_init(layer_idx):
    return 0.8 - 0.6 * math.exp(-0.3 * layer_idx)


def kernel(x, positions, attn_norm_w, w_in, gla_gate_w2, gla_gate_b, gla_out_norm_w,
           diff_q_norm_w, diff_k_norm_w, diff_lambda_q1, diff_lambda_k1, diff_lambda_q2,
           diff_lambda_k2, diff_subln_w, w_out, mlp_norm_w, w_up, w_down):
    batch, seq, d = x.shape
    depth = w_in.shape[0]
    m = batch * seq
    assert d == GLA_WIDTH + DIFF_WIDTH and seq % GLA_CHUNK == 0 and seq % Q_TILE == 0
    tm_proj = min(1024, m)
    tm_out = min(512, m)
    tm_mlp = min(512, m)

    half = DIFF_HEAD_DIM // 2
    inv_freq = ROPE_THETA ** (-jnp.arange(0, DIFF_HEAD_DIM, 2, dtype=F32) / DIFF_HEAD_DIM)
    ang = positions.astype(F32)[..., None] * inv_freq
    cos, sin = jnp.cos(ang), jnp.sin(ang)
    cosf = jnp.concatenate([cos, cos], axis=-1).reshape(m, 2 * half)
    sinf = jnp.concatenate([-sin, sin], axis=-1).reshape(m, 2 * half)

    wstack_np, level_np = _gla_constants()
    wstack = jnp.asarray(wstack_np, dtype=BF16)
    level = jnp.asarray(level_np)

    sizes = (GLA_KEY_WIDTH, GLA_KEY_WIDTH, GLA_WIDTH, GLA_WIDTH, GLA_GATE_RANK,
             DIFF_WIDTH, DIFF_WIDTH, DIFF_WIDTH)
    offs = np.concatenate([[0], np.cumsum(sizes)])
    lr0, lr1 = int(offs[4]), int(offs[5])

    w_main = jnp.concatenate([w_in[:, :, :lr0], w_in[:, :, lr1:]], axis=2).astype(BF16)
    w_lr = jnp.pad(w_in[:, :, lr0:lr1],
                   ((0, 0), (0, 0), (0, V7X_LANES - GLA_GATE_RANK))).astype(BF16)
    w_out_b = w_out.astype(BF16)
    w_up_b = w_up.astype(BF16)
    w_down_b = w_down.astype(BF16)

    x2 = x.reshape(m, d)
    for layer in range(depth):
        w2p = jnp.pad(gla_gate_w2[layer], ((0, V7X_LANES - GLA_GATE_RANK), (0, 0))).astype(BF16)

        proj, lr = _in_proj(x2, attn_norm_w[layer][None, :], w_main, w_lr, layer, tm_proj, 1024)
        y_gla = _gla(proj, lr, w2p, gla_gate_b[layer][None, :], wstack, level,
                     gla_out_norm_w[layer][None, :], batch, seq)
        y_diff = _diff_attn(proj, cosf, sinf,
                            diff_q_norm_w[layer][None, :], diff_k_norm_w[layer][None, :],
                            jnp.roll(diff_q_norm_w[layer], DIFF_HEAD_DIM // 2)[None, :],
                            jnp.roll(diff_k_norm_w[layer], DIFF_HEAD_DIM // 2)[None, :],
                            diff_lambda_q1[layer][None, :], diff_lambda_k1[layer][None, :],
                            diff_lambda_q2[layer][None, :], diff_lambda_k2[layer][None, :],
                            diff_subln_w[layer][None, :], batch, seq, _lambda_init(layer))
        x2 = _out_proj(x2, y_gla, y_diff, w_out_b, layer, tm_out)
        x2 = _mlp(x2, mlp_norm_w[layer][None, :], w_up_b, w_down_b, layer, tm_mlp, 1024)
    return x2.reshape(batch, seq, d)
```

```python
import functools
import math

import numpy as np
import jax
import jax.numpy as jnp
from jax import lax
from jax.experimental import pallas as pl
from jax.experimental.pallas import tpu as pltpu

F32 = jnp.float32
BF16 = jnp.bfloat16

GLA_HEADS = 4
GLA_HEAD_K = 128
GLA_HEAD_V = 256
GLA_KEY_WIDTH = GLA_HEADS * GLA_HEAD_K
GLA_WIDTH = GLA_HEADS * GLA_HEAD_V
GLA_GATE_RANK = 16
GLA_GATE_TAU = 16.0
DIFF_HEADS = 4
DIFF_HEAD_DIM = 128
DIFF_WIDTH = DIFF_HEADS * 2 * DIFF_HEAD_DIM
ROPE_THETA = 10000.0
NORM_EPS = 1e-6
SUBLN_EPS = 1e-5
LOG2E = math.log2(math.e)

V7X_LANES = 128
V7X_VMEM_BYTES = 64 * 1024 * 1024
VMEM_LIMIT_BYTES = 56 * 1024 * 1024

COL_GQ, COL_GK, COL_GV, COL_GG = 0, 512, 1024, 2048
COL_DQ, COL_DK, COL_DV = 3072, 4096, 5120
MAIN_COLS = 6144

GLA_CHUNK = 256
GLA_LEVELS = int(math.log2(GLA_CHUNK))
GLA_FAST_MAX_DECAY = 64.0
Q_TILE = 256
ROW_CHUNK = 256
SCORE_BOUND_NO_SHIFT = 60.0


def _cparams(sem):
    return pltpu.CompilerParams(dimension_semantics=sem, vmem_limit_bytes=VMEM_LIMIT_BYTES)


def _in_proj_kernel(x_ref, nw_ref, wa_ref, wb_ref, wlr_ref, o_ref, olr_ref, n_ref, *, na):
    j = pl.program_id(1)
    tm = x_ref.shape[0]

    @pl.when(j == 0)
    def _():
        for r0 in range(0, tm, ROW_CHUNK):
            x = x_ref[r0:r0 + ROW_CHUNK, :]
            ms = jnp.mean(x * x, axis=-1, keepdims=True)
            n = (x * lax.rsqrt(ms + NORM_EPS) * nw_ref[...]).astype(BF16)
            n_ref[r0:r0 + ROW_CHUNK, :] = n
            olr_ref[r0:r0 + ROW_CHUNK, :] = jnp.dot(
                n, wlr_ref[...], preferred_element_type=F32).astype(BF16)
            o_ref[r0:r0 + ROW_CHUNK, :] = jnp.dot(
                n, wa_ref[...], preferred_element_type=F32).astype(BF16)

    @pl.when(jnp.logical_and(j > 0, j < na))
    def _():
        o_ref[...] = jnp.dot(n_ref[...], wa_ref[...], preferred_element_type=F32).astype(BF16)

    @pl.when(j >= na)
    def _():
        o_ref[...] = jnp.dot(n_ref[...], wb_ref[...], preferred_element_type=F32).astype(BF16)


def _in_proj(x2, norm_w, w_a, w_b, w_lr, layer, tm, tn):
    m, d = x2.shape
    na = w_a.shape[2] // tn
    return pl.pallas_call(
        functools.partial(_in_proj_kernel, na=na),
        grid=(m // tm, MAIN_COLS // tn),
        in_specs=[
            pl.BlockSpec((tm, d), lambda i, j: (i, 0)),
            pl.BlockSpec((1, d), lambda i, j: (0, 0)),
            pl.BlockSpec((None, d, tn), lambda i, j: (layer, 0, jnp.minimum(j, na - 1))),
            pl.BlockSpec((None, d, tn), lambda i, j: (layer, 0, jnp.maximum(j - na, 0))),
            pl.BlockSpec((None, d, V7X_LANES), lambda i, j: (layer, 0, 0)),
        ],
        out_specs=[
            pl.BlockSpec((tm, tn), lambda i, j: (i, j)),
            pl.BlockSpec((tm, V7X_LANES), lambda i, j: (i, 0)),
        ],
        out_shape=[
            jax.ShapeDtypeStruct((m, MAIN_COLS), BF16),
            jax.ShapeDtypeStruct((m, V7X_LANES), BF16),
        ],
        scratch_shapes=[pltpu.VMEM((tm, d), BF16)],
        compiler_params=_cparams(("parallel", "arbitrary")),
        name="in_proj",
    )(x2, norm_w, w_a, w_b, w_lr)


def _gla_constants():
    c = GLA_CHUNK
    i = np.arange(c)[:, None]
    t = np.arange(c)[None, :]
    mats = [(t <= i), (t > i)]
    for lvl in range(GLA_LEVELS):
        s = 1 << lvl
        m = (i // (2 * s)) * (2 * s) + s - 1
        mats.append((t > np.minimum(i, m)) & (t <= np.maximum(i, m)))
    wstack = np.concatenate(mats, axis=0).astype(np.float32)
    x = i ^ t
    level = np.where(i > t, np.floor(np.log2(np.maximum(x, 1))), np.where(i == t, -1, -2))
    return wstack, level.astype(np.int32)


def _gla_kernel(q_ref, k_ref, v_ref, g_ref, lr_ref, w2_ref, gb_ref, wstk_ref, lvl_ref, onw_ref,
                o_ref, state_ref, b2_ref, *, seq):
    c = GLA_CHUNK
    dk = GLA_HEAD_K
    n_chunks = seq // c
    w2 = w2_ref[...]
    gb = gb_ref[...]
    onw = onw_ref[...]
    eye = (lax.broadcasted_iota(jnp.int32, (dk, dk), 0)
           == lax.broadcasted_iota(jnp.int32, (dk, dk), 1))
    nt = (((1,), (1,)), ((), ()))
    tn = (((0,), (0,)), ((), ()))
    norm_eps = NORM_EPS * dk

    def log_decay(r0):
        z = jnp.dot(lr_ref[pl.ds(r0, c), :], w2, preferred_element_type=F32) + gb
        la = (jnp.minimum(z, 0.0) - jnp.log1p(jnp.exp(-jnp.abs(z)))) * (1.0 / GLA_GATE_TAU)
        la_hi = la.astype(BF16)
        la_lo = (la - la_hi.astype(F32)).astype(BF16)
        return jnp.concatenate([la_hi, la_lo], axis=1)

    def finish(r0, o, kd, vb, e_last, state):
        g = g_ref[pl.ds(r0, c), :].astype(F32)
        y = o * lax.rsqrt(jnp.mean(o * o, axis=-1, keepdims=True) + norm_eps) * onw
        y = y * (g * jax.nn.sigmoid(g))
        o_ref[pl.ds(r0, c), :] = y.astype(o_ref.dtype)
        e_col = jnp.sum(jnp.where(eye, e_last, 0.0), axis=1, keepdims=True)
        state_ref[...] = state * e_col + lax.dot_general(kd, vb, tn, preferred_element_type=F32)

    def prepass(ci, tot):
        r0 = pl.multiple_of(ci * c, c)
        cs = jnp.dot(wstk_ref[0:c, :], log_decay(r0), preferred_element_type=F32)
        b2 = (cs[:, :dk] + cs[:, dk:]) * LOG2E
        b2_ref[pl.ds(r0, c), :] = b2
        return jnp.maximum(tot, -b2[c - 1:c, :])

    tot = lax.fori_loop(0, n_chunks, prepass, jnp.zeros((1, dk), F32), unroll=4)
    bounded = jnp.max(tot, axis=-1, keepdims=True)[0, 0] <= GLA_FAST_MAX_DECAY
    state_ref[...] = jnp.zeros_like(state_ref)

    def fast_chunk(ci, carry):
        r0 = pl.multiple_of(ci * c, c)
        qf = q_ref[pl.ds(r0, c), :].astype(F32)
        kf = k_ref[pl.ds(r0, c), :].astype(F32)
        vb = v_ref[pl.ds(r0, c), :]
        b2 = b2_ref[pl.ds(r0, c), :]
        eq = jnp.exp2(b2)
        qt = (qf * eq).astype(BF16)
        kt = kf * jnp.exp2(-b2)
        state = state_ref[...]
        causal = (lax.broadcasted_iota(jnp.int32, (c, c), 0)
                  >= lax.broadcasted_iota(jnp.int32, (c, c), 1))
        p = lax.dot_general(qt, kt.astype(BF16), nt, preferred_element_type=F32)
        o = jnp.dot(qt, state.astype(BF16), preferred_element_type=F32)
        o = o + jnp.dot(jnp.where(causal, p, 0.0).astype(BF16), vb, preferred_element_type=F32)
        e_last = eq[c - 1:c, :]
        finish(r0, o, (kt * e_last).astype(BF16), vb, e_last, state)
        return carry

    def level_chunk(ci, carry):
        r0 = pl.multiple_of(ci * c, c)
        lvl = lvl_ref[...]
        row = lax.broadcasted_iota(jnp.int32, (c, 1), 0)
        qf = q_ref[pl.ds(r0, c), :].astype(F32)
        kf = k_ref[pl.ds(r0, c), :].astype(F32)
        kb = k_ref[pl.ds(r0, c), :]
        vb = v_ref[pl.ds(r0, c), :]
        sums = jnp.dot(wstk_ref[...], log_decay(r0), preferred_element_type=F32)
        sums = sums[:, :dk] + sums[:, dk:]
        b = sums[0:c]
        b_rev = sums[c:2 * c]
        state = state_ref[...]
        o = jnp.dot((qf * jnp.exp(b)).astype(BF16), state.astype(BF16),
                    preferred_element_type=F32)
        attn = jnp.where(lvl == -1,
                         lax.dot_general(qf.astype(BF16), kb, nt, preferred_element_type=F32), 0.0)
        for l in range(GLA_LEVELS):
            e = jnp.exp(sums[(l + 2) * c:(l + 3) * c])
            upper = ((row >> l) & 1) == 1
            qt = jnp.where(upper, qf * e, 0.0).astype(BF16)
            kt = jnp.where(upper, 0.0, kf * e).astype(BF16)
            p = lax.dot_general(qt, kt, nt, preferred_element_type=F32)
            attn = jnp.where(lvl == l, p, attn)
        o = o + jnp.dot(attn.astype(BF16), vb, preferred_element_type=F32)
        finish(r0, o, (kf * jnp.exp(b_rev)).astype(BF16), vb, jnp.exp(b[c - 1:c, :]), state)
        return carry

    @pl.when(bounded)
    def _():
        lax.fori_loop(0, n_chunks, fast_chunk, 0, unroll=4)

    @pl.when(jnp.logical_not(bounded))
    def _():
        lax.fori_loop(0, n_chunks, level_chunk, 0)


def _gla(proj, lr, w2p, gate_b, wstack, level, out_norm_w, batch, seq):
    m = proj.shape[0]
    kq = GLA_HEAD_K
    kv = GLA_HEAD_V
    rows = wstack.shape[0]
    return pl.pallas_call(
        functools.partial(_gla_kernel, seq=seq),
        grid=(batch, GLA_HEADS),
        in_specs=[
            pl.BlockSpec((seq, kq), lambda b, h: (b, COL_GQ // kq + h)),
            pl.BlockSpec((seq, kq), lambda b, h: (b, COL_GK // kq + h)),
            pl.BlockSpec((seq, kv), lambda b, h: (b, COL_GV // kv + h)),
            pl.BlockSpec((seq, kv), lambda b, h: (b, COL_GG // kv + h)),
            pl.BlockSpec((seq, V7X_LANES), lambda b, h: (b, 0)),
            pl.BlockSpec((V7X_LANES, kq), lambda b, h: (0, h)),
            pl.BlockSpec((1, kq), lambda b, h: (0, h)),
            pl.BlockSpec((rows, GLA_CHUNK), lambda b, h: (0, 0)),
            pl.BlockSpec((GLA_CHUNK, GLA_CHUNK), lambda b, h: (0, 0)),
            pl.BlockSpec((1, kv), lambda b, h: (0, 0)),
        ],
        out_specs=pl.BlockSpec((seq, kv), lambda b, h: (b, h)),
        out_shape=jax.ShapeDtypeStruct((m, GLA_WIDTH), BF16),
        scratch_shapes=[pltpu.VMEM((kq, kv), F32), pltpu.VMEM((seq, kq), F32)],
        compiler_params=_cparams(("parallel", "parallel")),
        name="gla",
    )(proj, proj, proj, proj, lr, w2p, gate_b, wstack, level, out_norm_w)


def _diff_kernel(q_ref, k_ref, v_ref, cos_ref, sin_ref, qnw_ref, knw_ref, qnr_ref, knr_ref,
                 lq1_ref, lk1_ref, lq2_ref, lk2_ref, sw_ref, o_ref, qs_ref, ks_ref, tab_ref,
                 *, seq, lam_init):
    dh = DIFF_HEAD_DIM
    tq = Q_TILE
    lam = (jnp.exp(jnp.sum(lq1_ref[...] * lk1_ref[...], axis=-1, keepdims=True))
           - jnp.exp(jnp.sum(lq2_ref[...] * lk2_ref[...], axis=-1, keepdims=True))
           + lam_init)
    qscale = (dh ** -0.5) * LOG2E

    @pl.when(pl.program_id(1) == 0)
    def _():
        cosf = jnp.concatenate([cos_ref[...], cos_ref[...]], axis=1)
        sinf = jnp.concatenate([-sin_ref[...], sin_ref[...]], axis=1)
        tab_ref[0] = cosf * (qnw_ref[...] * qscale)
        tab_ref[1] = sinf * (qnr_ref[...] * qscale)
        tab_ref[2] = cosf * knw_ref[...]
        tab_ref[3] = sinf * knr_ref[...]

    perm = (lax.broadcasted_iota(jnp.int32, (dh, dh), 0)
            == ((lax.broadcasted_iota(jnp.int32, (dh, dh), 1) + dh // 2) % dh)).astype(BF16)

    def prep(src_ref, tab, comp, r0, rows):
        tb = src_ref[r0:r0 + rows, comp * dh:(comp + 1) * dh]
        t = tb.astype(F32)
        r = lax.rsqrt(jnp.mean(t * t, axis=-1, keepdims=True) + NORM_EPS)
        tr = jnp.dot(tb, perm, preferred_element_type=F32)
        return (r * (t * tab_ref[tab, r0:r0 + rows, :]
                     + tr * tab_ref[tab + 1, r0:r0 + rows, :])).astype(BF16)

    for comp in range(2):
        qs_ref[comp] = prep(q_ref, 0, comp, 0, seq)
        ks_ref[comp] = prep(k_ref, 2, comp, 0, seq)

    nt = (((1,), (1,)), ((), ()))
    tri = (lax.broadcasted_iota(jnp.int32, (tq, tq), 0)
           >= lax.broadcasted_iota(jnp.int32, (tq, tq), 1))
    sw = sw_ref[...] * (1.0 - lam_init)

    def attend(subtract_max):
        for t in reversed(range(seq // tq)):
            r0 = t * tq
            ps, ls = [], []
            for comp in range(2):
                q = qs_ref[comp, r0:r0 + tq, :]
                s = lax.dot_general(q, ks_ref[comp, r0:r0 + tq, :], nt,
                                    preferred_element_type=F32)
                s = jnp.where(tri, s, -jnp.inf)
                if t > 0:
                    s_off = lax.dot_general(q, ks_ref[comp, 0:r0, :], nt,
                                            preferred_element_type=F32)
                    s = jnp.concatenate([s_off, s], axis=1)
                if subtract_max:
                    s = s - jnp.max(s, axis=-1, keepdims=True)
                p = jnp.exp2(s)
                ls.append(jnp.sum(p, axis=-1, keepdims=True))
                ps.append(p.astype(BF16))
            ratio = (lam * ls[0] / ls[1]).astype(BF16)
            w = ps[0] - ratio * ps[1]
            o = jnp.dot(w, v_ref[0:r0 + tq, :], preferred_element_type=F32) * (1.0 / ls[0])
            o = o * lax.rsqrt(jnp.mean(o * o, axis=-1, keepdims=True) + SUBLN_EPS) * sw
            o_ref[r0:r0 + tq, :] = o.astype(o_ref.dtype)

    bound = (dh * qscale) * (jnp.max(jnp.abs(qnw_ref[...]), axis=-1, keepdims=True)
                             * jnp.max(jnp.abs(knw_ref[...]), axis=-1, keepdims=True))
    small = bound[0, 0] <= SCORE_BOUND_NO_SHIFT

    @pl.when(small)
    def _():
        attend(False)

    @pl.when(jnp.logical_not(small))
    def _():
        attend(True)


def _diff_attn(proj, cos, sin, qnw, knw, qnr, knr, lq1, lk1, lq2, lk2, subln_w, batch, seq,
               lam_init):
    m = proj.shape[0]
    hw = 2 * DIFF_HEAD_DIM
    vec = pl.BlockSpec((1, DIFF_HEAD_DIM), lambda b, h: (0, 0))
    return pl.pallas_call(
        functools.partial(_diff_kernel, seq=seq, lam_init=lam_init),
        grid=(batch, DIFF_HEADS),
        in_specs=[
            pl.BlockSpec((seq, hw), lambda b, h: (b, COL_DQ // hw + h)),
            pl.BlockSpec((seq, hw), lambda b, h: (b, COL_DK // hw + h)),
            pl.BlockSpec((seq, hw), lambda b, h: (b, COL_DV // hw + h)),
            pl.BlockSpec((seq, DIFF_HEAD_DIM // 2), lambda b, h: (b, 0)),
            pl.BlockSpec((seq, DIFF_HEAD_DIM // 2), lambda b, h: (b, 0)),
            vec, vec, vec, vec, vec, vec, vec, vec,
            pl.BlockSpec((1, hw), lambda b, h: (0, 0)),
        ],
        out_specs=pl.BlockSpec((seq, hw), lambda b, h: (b, h)),
        out_shape=jax.ShapeDtypeStruct((m, DIFF_WIDTH), BF16),
        scratch_shapes=[pltpu.VMEM((2, seq, DIFF_HEAD_DIM), BF16),
                        pltpu.VMEM((2, seq, DIFF_HEAD_DIM), BF16),
                        pltpu.VMEM((4, seq, DIFF_HEAD_DIM), F32)],
        compiler_params=_cparams(("parallel", "arbitrary")),
        name="diff_attn",
    )(proj, proj, proj, cos, sin, qnw, knw, qnr, knr, lq1, lk1, lq2, lk2, subln_w)


def _out_proj_kernel(x_ref, yg_ref, yd_ref, wg_ref, wd_ref, o_ref):
    acc = jnp.dot(yg_ref[...], wg_ref[...], preferred_element_type=F32)
    acc = acc + jnp.dot(yd_ref[...], wd_ref[...], preferred_element_type=F32)
    o_ref[...] = x_ref[...] + acc


def _out_proj(x2, yg, yd, w_out, layer, tm):
    m, d = x2.shape
    return pl.pallas_call(
        _out_proj_kernel,
        grid=(m // tm,),
        in_specs=[
            pl.BlockSpec((tm, d), lambda i: (i, 0)),
            pl.BlockSpec((tm, GLA_WIDTH), lambda i: (i, 0)),
            pl.BlockSpec((tm, DIFF_WIDTH), lambda i: (i, 0)),
            pl.BlockSpec((None, GLA_WIDTH, d), lambda i: (layer, 0, 0)),
            pl.BlockSpec((None, DIFF_WIDTH, d), lambda i: (layer, GLA_WIDTH // DIFF_WIDTH, 0)),
        ],
        out_specs=pl.BlockSpec((tm, d), lambda i: (i, 0)),
        out_shape=jax.ShapeDtypeStruct((m, d), F32),
        compiler_params=_cparams(("parallel",)),
        name="out_proj",
    )(x2, yg, yd, w_out, w_out)


def _mlp_kernel(x_ref, nw_ref, wu_ref, wd_ref, o_ref, n_ref):
    j = pl.program_id(1)
    tm = x_ref.shape[0]

    def ff(n):
        h = jnp.dot(n, wu_ref[...], preferred_element_type=F32)
        h = jnp.square(jnp.maximum(h, 0.0)).astype(BF16)
        return jnp.dot(h, wd_ref[...], preferred_element_type=F32)

    @pl.when(j == 0)
    def _():
        for r0 in range(0, tm, ROW_CHUNK):
            x = x_ref[r0:r0 + ROW_CHUNK, :]
            ms = jnp.mean(x * x, axis=-1, keepdims=True)
            n = (x * lax.rsqrt(ms + NORM_EPS) * nw_ref[...]).astype(BF16)
            n_ref[r0:r0 + ROW_CHUNK, :] = n
            o_ref[r0:r0 + ROW_CHUNK, :] = x + ff(n)

    @pl.when(j > 0)
    def _():
        o_ref[...] += ff(n_ref[...])


def _mlp(x2, norm_w, w_up, w_down, layer, tm, tf):
    m, d = x2.shape
    f = w_up.shape[2]
    return pl.pallas_call(
        _mlp_kernel,
        grid=(m // tm, f // tf),
        in_specs=[
            pl.BlockSpec((tm, d), lambda i, j: (i, 0)),
            pl.BlockSpec((1, d), lambda i, j: (0, 0)),
            pl.BlockSpec((None, d, tf), lambda i, j: (layer, 0, j)),
            pl.BlockSpec((None, tf, d), lambda i, j: (layer, j, 0)),
        ],
        out_specs=pl.BlockSpec((tm, d), lambda i, j: (i, 0)),
        out_shape=jax.ShapeDtypeStruct((m, d), F32),
        scratch_shapes=[pltpu.VMEM((tm, d), BF16)],
        compiler_params=_cparams(("parallel", "arbitrary")),
        name="mlp",
    )(x2, norm_w, w_up, w_down)


def _lambda_init(layer_idx):
    return 0.8 - 0.6 * math.exp(-0.3 * layer_idx)


def kernel(x, positions, attn_norm_w, w_in, gla_gate_w2, gla_gate_b, gla_out_norm_w,
           diff_q_norm_w, diff_k_norm_w, diff_lambda_q1, diff_lambda_k1, diff_lambda_q2,
           diff_lambda_k2, diff_subln_w, w_out, mlp_norm_w, w_up, w_down):
    batch, seq, d = x.shape
    depth = w_in.shape[0]
    m = batch * seq
    assert d == GLA_WIDTH + DIFF_WIDTH and seq % GLA_CHUNK == 0 and seq % Q_TILE == 0
    tm_proj = min(1024, m)
    tm_out = min(512, m)
    tm_mlp = min(512, m)

    half = DIFF_HEAD_DIM // 2
    inv_freq = ROPE_THETA ** (-jnp.arange(0, DIFF_HEAD_DIM, 2, dtype=F32) / DIFF_HEAD_DIM)
    ang = (positions.astype(F32)[..., None] * inv_freq).reshape(m, half)
    cos, sin = jnp.cos(ang), jnp.sin(ang)

    wstack_np, level_np = _gla_constants()
    wstack = jnp.asarray(wstack_np, dtype=BF16)
    level = jnp.asarray(level_np)

    sizes = (GLA_KEY_WIDTH, GLA_KEY_WIDTH, GLA_WIDTH, GLA_WIDTH, GLA_GATE_RANK,
             DIFF_WIDTH, DIFF_WIDTH, DIFF_WIDTH)
    offs = np.concatenate([[0], np.cumsum(sizes)])
    lr0, lr1 = int(offs[4]), int(offs[5])

    w_a = w_in[:, :, :lr0].astype(BF16)
    w_b = w_in[:, :, lr1:].astype(BF16)
    w_lr = jnp.pad(w_in[:, :, lr0:lr1],
                   ((0, 0), (0, 0), (0, V7X_LANES - GLA_GATE_RANK))).astype(BF16)
    w_out_b = w_out.astype(BF16)
    w_up_b = w_up.astype(BF16)
    w_down_b = w_down.astype(BF16)

    x2 = x.reshape(m, d)
    for layer in range(depth):
        w2p = jnp.pad(gla_gate_w2[layer], ((0, V7X_LANES - GLA_GATE_RANK), (0, 0))).astype(BF16)

        proj, lr = _in_proj(x2, attn_norm_w[layer][None, :], w_a, w_b, w_lr, layer, tm_proj, 1024)
        y_gla = _gla(proj, lr, w2p, gla_gate_b[layer][None, :], wstack, level,
                     gla_out_norm_w[layer][None, :], batch, seq)
        y_diff = _diff_attn(proj, cos, sin,
                            diff_q_norm_w[layer][None, :], diff_k_norm_w[layer][None, :],
                            jnp.roll(diff_q_norm_w[layer], DIFF_HEAD_DIM // 2)[None, :],
                            jnp.roll(diff_k_norm_w[layer], DIFF_HEAD_DIM // 2)[None, :],
                            diff_lambda_q1[layer][None, :], diff_lambda_k1[layer][None, :],
                            diff_lambda_q2[layer][None, :], diff_lambda_k2[layer][None, :],
                            diff_subln_w[layer][None, :], batch, seq, _lambda_init(layer))
        x2 = _out_proj(x2, y_gla, y_diff, w_out_b, layer, tm_out)
        x2 = _mlp(x2, mlp_norm_w[layer][None, :], w_up_b, w_down_b, layer, tm_mlp, 1024)
    return x2.reshape(batch, seq, d)
```

```python
import functools
import math

import numpy as np
import jax
import jax.numpy as jnp
from jax import lax
from jax.experimental import pallas as pl
from jax.experimental.pallas import tpu as pltpu

F32 = jnp.float32
BF16 = jnp.bfloat16

GLA_HEADS = 4
GLA_HEAD_K = 128
GLA_HEAD_V = 256
GLA_KEY_WIDTH = GLA_HEADS * GLA_HEAD_K
GLA_WIDTH = GLA_HEADS * GLA_HEAD_V
GLA_GATE_RANK = 16
GLA_GATE_TAU = 16.0
DIFF_HEADS = 4
DIFF_HEAD_DIM = 128
DIFF_WIDTH = DIFF_HEADS * 2 * DIFF_HEAD_DIM
ROPE_THETA = 10000.0
NORM_EPS = 1e-6
SUBLN_EPS = 1e-5
LOG2E = math.log2(math.e)

V7X_LANES = 128
V7X_VMEM_BYTES = 64 * 1024 * 1024
VMEM_LIMIT_BYTES = 56 * 1024 * 1024

COL_GQ, COL_GK, COL_GV, COL_GG = 0, 512, 1024, 2048
COL_DQ, COL_DK, COL_DV = 3072, 4096, 5120
MAIN_COLS = 6144

GLA_CHUNK = 256
GLA_LEVELS = int(math.log2(GLA_CHUNK))
GLA_FAST_MAX_DECAY = 64.0
Q_TILE = 256
ROW_CHUNK = 256
SCORE_BOUND_NO_SHIFT = 60.0


def _cparams(sem):
    return pltpu.CompilerParams(dimension_semantics=sem, vmem_limit_bytes=VMEM_LIMIT_BYTES)


def _in_proj_kernel(x_ref, nw_ref, wa_ref, wb_ref, wlr_ref, o_ref, olr_ref, n_ref, *, na):
    j = pl.program_id(1)
    tm = x_ref.shape[0]

    @pl.when(j == 0)
    def _():
        for r0 in range(0, tm, ROW_CHUNK):
            x = x_ref[r0:r0 + ROW_CHUNK, :]
            ms = jnp.mean(x * x, axis=-1, keepdims=True)
            n = (x * lax.rsqrt(ms + NORM_EPS) * nw_ref[...]).astype(BF16)
            n_ref[r0:r0 + ROW_CHUNK, :] = n
            olr_ref[r0:r0 + ROW_CHUNK, :] = jnp.dot(
                n, wlr_ref[...], preferred_element_type=F32).astype(BF16)
            o_ref[r0:r0 + ROW_CHUNK, :] = jnp.dot(
                n, wa_ref[...], preferred_element_type=F32).astype(BF16)

    @pl.when(jnp.logical_and(j > 0, j < na))
    def _():
        o_ref[...] = jnp.dot(n_ref[...], wa_ref[...], preferred_element_type=F32).astype(BF16)

    @pl.when(j >= na)
    def _():
        o_ref[...] = jnp.dot(n_ref[...], wb_ref[...], preferred_element_type=F32).astype(BF16)


def _in_proj(x2, norm_w, w_a, w_b, w_lr, layer, tm, tn):
    m, d = x2.shape
    na = w_a.shape[2] // tn
    nb = w_b.shape[2] // tn
    return pl.pallas_call(
        functools.partial(_in_proj_kernel, na=na),
        grid=(m // tm, MAIN_COLS // tn),
        in_specs=[
            pl.BlockSpec((tm, d), lambda i, j: (i, 0)),
            pl.BlockSpec((1, d), lambda i, j: (0, 0)),
            pl.BlockSpec((None, d, tn), lambda i, j: (layer, 0, jnp.minimum(j, na - 1))),
            pl.BlockSpec((None, d, tn), lambda i, j: (
                layer, 0, jnp.where(j < na - 1, nb - 1, jnp.maximum(j - na, 0)))),
            pl.BlockSpec((None, d, V7X_LANES), lambda i, j: (layer, 0, 0)),
        ],
        out_specs=[
            pl.BlockSpec((tm, tn), lambda i, j: (i, j)),
            pl.BlockSpec((tm, V7X_LANES), lambda i, j: (i, 0)),
        ],
        out_shape=[
            jax.ShapeDtypeStruct((m, MAIN_COLS), BF16),
            jax.ShapeDtypeStruct((m, V7X_LANES), BF16),
        ],
        scratch_shapes=[pltpu.VMEM((tm, d), BF16)],
        compiler_params=_cparams(("parallel", "arbitrary")),
        name="in_proj",
    )(x2, norm_w, w_a, w_b, w_lr)


def _gla_constants():
    c = GLA_CHUNK
    i = np.arange(c)[:, None]
    t = np.arange(c)[None, :]
    mats = [(t <= i), (t > i)]
    for lvl in range(GLA_LEVELS):
        s = 1 << lvl
        m = (i // (2 * s)) * (2 * s) + s - 1
        mats.append((t > np.minimum(i, m)) & (t <= np.maximum(i, m)))
    wstack = np.concatenate(mats, axis=0).astype(np.float32)
    x = i ^ t
    level = np.where(i > t, np.floor(np.log2(np.maximum(x, 1))), np.where(i == t, -1, -2))
    return wstack, level.astype(np.int32)


def _gla_kernel(q_ref, k_ref, v_ref, g_ref, lr_ref, w2_ref, gb_ref, wstk_ref, lvl_ref, onw_ref,
                o_ref, state_ref, b2_ref, *, seq):
    c = GLA_CHUNK
    dk = GLA_HEAD_K
    n_chunks = seq // c
    w2 = w2_ref[...]
    gb = gb_ref[...]
    onw = onw_ref[...]
    eye = (lax.broadcasted_iota(jnp.int32, (dk, dk), 0)
           == lax.broadcasted_iota(jnp.int32, (dk, dk), 1))
    nt = (((1,), (1,)), ((), ()))
    tn = (((0,), (0,)), ((), ()))
    norm_eps = NORM_EPS * dk

    def log_decay(r0):
        z = jnp.dot(lr_ref[pl.ds(r0, c), :], w2, preferred_element_type=F32) + gb
        la = (jnp.minimum(z, 0.0) - jnp.log1p(jnp.exp(-jnp.abs(z)))) * (1.0 / GLA_GATE_TAU)
        la_hi = la.astype(BF16)
        la_lo = (la - la_hi.astype(F32)).astype(BF16)
        return jnp.concatenate([la_hi, la_lo], axis=1)

    def finish(r0, o, kd, vb, e_last, state):
        g = g_ref[pl.ds(r0, c), :].astype(F32)
        y = o * lax.rsqrt(jnp.mean(o * o, axis=-1, keepdims=True) + norm_eps) * onw
        y = y * (g * jax.nn.sigmoid(g))
        o_ref[pl.ds(r0, c), :] = y.astype(o_ref.dtype)
        e_col = jnp.sum(jnp.where(eye, e_last, 0.0), axis=1, keepdims=True)
        state_ref[...] = state * e_col + lax.dot_general(kd, vb, tn, preferred_element_type=F32)

    def prepass(ci, tot):
        r0 = pl.multiple_of(ci * c, c)
        cs = jnp.dot(wstk_ref[0:c, :], log_decay(r0), preferred_element_type=F32)
        b2 = (cs[:, :dk] + cs[:, dk:]) * LOG2E
        b2_ref[pl.ds(r0, c), :] = b2
        return jnp.maximum(tot, -b2[c - 1:c, :])

    tot = lax.fori_loop(0, n_chunks, prepass, jnp.zeros((1, dk), F32), unroll=4)
    bounded = jnp.max(tot, axis=-1, keepdims=True)[0, 0] <= GLA_FAST_MAX_DECAY
    state_ref[...] = jnp.zeros_like(state_ref)

    def fast_chunk(ci, carry):
        r0 = pl.multiple_of(ci * c, c)
        qf = q_ref[pl.ds(r0, c), :].astype(F32)
        kf = k_ref[pl.ds(r0, c), :].astype(F32)
        vb = v_ref[pl.ds(r0, c), :]
        b2 = b2_ref[pl.ds(r0, c), :]
        eq = jnp.exp2(b2)
        qt = (qf * eq).astype(BF16)
        kt = kf * jnp.exp2(-b2)
        state = state_ref[...]
        causal = (lax.broadcasted_iota(jnp.int32, (c, c), 0)
                  >= lax.broadcasted_iota(jnp.int32, (c, c), 1))
        p = lax.dot_general(qt, kt.astype(BF16), nt, preferred_element_type=F32)
        o = jnp.dot(qt, state.astype(BF16), preferred_element_type=F32)
        o = o + jnp.dot(jnp.where(causal, p, 0.0).astype(BF16), vb, preferred_element_type=F32)
        e_last = eq[c - 1:c, :]
        finish(r0, o, (kt * e_last).astype(BF16), vb, e_last, state)
        return carry

    def level_chunk(ci, carry):
        r0 = pl.multiple_of(ci * c, c)
        lvl = lvl_ref[...]
        row = lax.broadcasted_iota(jnp.int32, (c, 1), 0)
        qf = q_ref[pl.ds(r0, c), :].astype(F32)
        kf = k_ref[pl.ds(r0, c), :].astype(F32)
        kb = k_ref[pl.ds(r0, c), :]
        vb = v_ref[pl.ds(r0, c), :]
        sums = jnp.dot(wstk_ref[...], log_decay(r0), preferred_element_type=F32)
        sums = sums[:, :dk] + sums[:, dk:]
        b = sums[0:c]
        b_rev = sums[c:2 * c]
        state = state_ref[...]
        o = jnp.dot((qf * jnp.exp(b)).astype(BF16), state.astype(BF16),
                    preferred_element_type=F32)
        attn = jnp.where(lvl == -1,
                         lax.dot_general(qf.astype(BF16), kb, nt, preferred_element_type=F32), 0.0)
        for l in range(GLA_LEVELS):
            e = jnp.exp(sums[(l + 2) * c:(l + 3) * c])
            upper = ((row >> l) & 1) == 1
            qt = jnp.where(upper, qf * e, 0.0).astype(BF16)
            kt = jnp.where(upper, 0.0, kf * e).astype(BF16)
            p = lax.dot_general(qt, kt, nt, preferred_element_type=F32)
            attn = jnp.where(lvl == l, p, attn)
        o = o + jnp.dot(attn.astype(BF16), vb, preferred_element_type=F32)
        finish(r0, o, (kf * jnp.exp(b_rev)).astype(BF16), vb, jnp.exp(b[c - 1:c, :]), state)
        return carry

    @pl.when(bounded)
    def _():
        lax.fori_loop(0, n_chunks, fast_chunk, 0, unroll=4)

    @pl.when(jnp.logical_not(bounded))
    def _():
        lax.fori_loop(0, n_chunks, level_chunk, 0)


def _gla(proj, lr, w2p, gate_b, wstack, level, out_norm_w, batch, seq):
    m = proj.shape[0]
    kq = GLA_HEAD_K
    kv = GLA_HEAD_V
    rows = wstack.shape[0]
    return pl.pallas_call(
        functools.partial(_gla_kernel, seq=seq),
        grid=(batch, GLA_HEADS),
        in_specs=[
            pl.BlockSpec((seq, kq), lambda b, h: (b, COL_GQ // kq + h)),
            pl.BlockSpec((seq, kq), lambda b, h: (b, COL_GK // kq + h)),
            pl.BlockSpec((seq, kv), lambda b, h: (b, COL_GV // kv + h)),
            pl.BlockSpec((seq, kv), lambda b, h: (b, COL_GG // kv + h)),
            pl.BlockSpec((seq, V7X_LANES), lambda b, h: (b, 0)),
            pl.BlockSpec((V7X_LANES, kq), lambda b, h: (0, h)),
            pl.BlockSpec((1, kq), lambda b, h: (0, h)),
            pl.BlockSpec((rows, GLA_CHUNK), lambda b, h: (0, 0)),
            pl.BlockSpec((GLA_CHUNK, GLA_CHUNK), lambda b, h: (0, 0)),
            pl.BlockSpec((1, kv), lambda b, h: (0, 0)),
        ],
        out_specs=pl.BlockSpec((seq, kv), lambda b, h: (b, h)),
        out_shape=jax.ShapeDtypeStruct((m, GLA_WIDTH), BF16),
        scratch_shapes=[pltpu.VMEM((kq, kv), F32), pltpu.VMEM((seq, kq), F32)],
        compiler_params=_cparams(("parallel", "parallel")),
        name="gla",
    )(proj, proj, proj, proj, lr, w2p, gate_b, wstack, level, out_norm_w)


def _diff_kernel(q_ref, k_ref, v_ref, cos_ref, sin_ref, qnw_ref, knw_ref, qnr_ref, knr_ref,
                 lq1_ref, lk1_ref, lq2_ref, lk2_ref, sw_ref, o_ref, qs_ref, ks_ref, tab_ref,
                 *, seq, lam_init):
    dh = DIFF_HEAD_DIM
    tq = Q_TILE
    lam = (jnp.exp(jnp.sum(lq1_ref[...] * lk1_ref[...], axis=-1, keepdims=True))
           - jnp.exp(jnp.sum(lq2_ref[...] * lk2_ref[...], axis=-1, keepdims=True))
           + lam_init)
    qscale = (dh ** -0.5) * LOG2E

    @pl.when(pl.program_id(1) == 0)
    def _():
        cosf = jnp.concatenate([cos_ref[...], cos_ref[...]], axis=1)
        sinf = jnp.concatenate([-sin_ref[...], sin_ref[...]], axis=1)
        tab_ref[0] = cosf * (qnw_ref[...] * qscale)
        tab_ref[1] = sinf * (qnr_ref[...] * qscale)
        tab_ref[2] = cosf * knw_ref[...]
        tab_ref[3] = sinf * knr_ref[...]

    perm = (lax.broadcasted_iota(jnp.int32, (dh, dh), 0)
            == ((lax.broadcasted_iota(jnp.int32, (dh, dh), 1) + dh // 2) % dh)).astype(BF16)

    def prep(src_ref, tab, comp, r0, rows):
        tb = src_ref[r0:r0 + rows, comp * dh:(comp + 1) * dh]
        t = tb.astype(F32)
        r = lax.rsqrt(jnp.mean(t * t, axis=-1, keepdims=True) + NORM_EPS)
        tr = jnp.dot(tb, perm, preferred_element_type=F32)
        return (r * (t * tab_ref[tab, r0:r0 + rows, :]
                     + tr * tab_ref[tab + 1, r0:r0 + rows, :])).astype(BF16)

    for comp in range(2):
        qs_ref[comp] = prep(q_ref, 0, comp, 0, seq)
        ks_ref[comp] = prep(k_ref, 2, comp, 0, seq)

    nt = (((1,), (1,)), ((), ()))
    tri = (lax.broadcasted_iota(jnp.int32, (tq, tq), 0)
           >= lax.broadcasted_iota(jnp.int32, (tq, tq), 1))
    sw = sw_ref[...] * (1.0 - lam_init)

    def attend(subtract_max):
        for t in reversed(range(seq // tq)):
            r0 = t * tq
            ps, ls = [], []
            for comp in range(2):
                q = qs_ref[comp, r0:r0 + tq, :]
                s = lax.dot_general(q, ks_ref[comp, r0:r0 + tq, :], nt,
                                    preferred_element_type=F32)
                s = jnp.where(tri, s, -jnp.inf)
                if t > 0:
                    s_off = lax.dot_general(q, ks_ref[comp, 0:r0, :], nt,
                                            preferred_element_type=F32)
                    s = jnp.concatenate([s_off, s], axis=1)
                if subtract_max:
                    s = s - jnp.max(s, axis=-1, keepdims=True)
                p = jnp.exp2(s)
                ls.append(jnp.sum(p, axis=-1, keepdims=True))
                ps.append(p.astype(BF16))
            ratio = (lam * ls[0] / ls[1]).astype(BF16)
            w = ps[0] - ratio * ps[1]
            o = jnp.dot(w, v_ref[0:r0 + tq, :], preferred_element_type=F32) * (1.0 / ls[0])
            o = o * lax.rsqrt(jnp.mean(o * o, axis=-1, keepdims=True) + SUBLN_EPS) * sw
            o_ref[r0:r0 + tq, :] = o.astype(o_ref.dtype)

    bound = (dh * qscale) * (jnp.max(jnp.abs(qnw_ref[...]), axis=-1, keepdims=True)
                             * jnp.max(jnp.abs(knw_ref[...]), axis=-1, keepdims=True))
    small = bound[0, 0] <= SCORE_BOUND_NO_SHIFT

    @pl.when(small)
    def _():
        attend(False)

    @pl.when(jnp.logical_not(small))
    def _():
        attend(True)


def _diff_attn(proj, cos, sin, qnw, knw, qnr, knr, lq1, lk1, lq2, lk2, subln_w, batch, seq,
               lam_init):
    m = proj.shape[0]
    hw = 2 * DIFF_HEAD_DIM
    vec = pl.BlockSpec((1, DIFF_HEAD_DIM), lambda b, h: (0, 0))
    return pl.pallas_call(
        functools.partial(_diff_kernel, seq=seq, lam_init=lam_init),
        grid=(batch, DIFF_HEADS),
        in_specs=[
            pl.BlockSpec((seq, hw), lambda b, h: (b, COL_DQ // hw + h)),
            pl.BlockSpec((seq, hw), lambda b, h: (b, COL_DK // hw + h)),
            pl.BlockSpec((seq, hw), lambda b, h: (b, COL_DV // hw + h)),
            pl.BlockSpec((seq, DIFF_HEAD_DIM // 2), lambda b, h: (b, 0)),
            pl.BlockSpec((seq, DIFF_HEAD_DIM // 2), lambda b, h: (b, 0)),
            vec, vec, vec, vec, vec, vec, vec, vec,
            pl.BlockSpec((1, hw), lambda b, h: (0, 0)),
        ],
        out_specs=pl.BlockSpec((seq, hw), lambda b, h: (b, h)),
        out_shape=jax.ShapeDtypeStruct((m, DIFF_WIDTH), BF16),
        scratch_shapes=[pltpu.VMEM((2, seq, DIFF_HEAD_DIM), BF16),
                        pltpu.VMEM((2, seq, DIFF_HEAD_DIM), BF16),
                        pltpu.VMEM((4, seq, DIFF_HEAD_DIM), F32)],
        compiler_params=_cparams(("parallel", "arbitrary")),
        name="diff_attn",
    )(proj, proj, proj, cos, sin, qnw, knw, qnr, knr, lq1, lk1, lq2, lk2, subln_w)


def _out_proj_kernel(x_ref, yg_ref, yd_ref, wg_ref, wd_ref, o_ref):
    acc = jnp.dot(yg_ref[...], wg_ref[...], preferred_element_type=F32)
    acc = acc + jnp.dot(yd_ref[...], wd_ref[...], preferred_element_type=F32)
    o_ref[...] = x_ref[...] + acc


def _out_proj(x2, yg, yd, w_out, layer, tm):
    m, d = x2.shape
    return pl.pallas_call(
        _out_proj_kernel,
        grid=(m // tm,),
        in_specs=[
            pl.BlockSpec((tm, d), lambda i: (i, 0)),
            pl.BlockSpec((tm, GLA_WIDTH), lambda i: (i, 0)),
            pl.BlockSpec((tm, DIFF_WIDTH), lambda i: (i, 0)),
            pl.BlockSpec((None, GLA_WIDTH, d), lambda i: (layer, 0, 0)),
            pl.BlockSpec((None, DIFF_WIDTH, d), lambda i: (layer, GLA_WIDTH // DIFF_WIDTH, 0)),
        ],
        out_specs=pl.BlockSpec((tm, d), lambda i: (i, 0)),
        out_shape=jax.ShapeDtypeStruct((m, d), F32),
        compiler_params=_cparams(("parallel",)),
        name="out_proj",
    )(x2, yg, yd, w_out, w_out)


def _mlp_kernel(x_ref, nw_ref, wu_ref, wd_ref, o_ref, n_ref):
    j = pl.program_id(1)
    tm = x_ref.shape[0]

    def ff(n):
        h = jnp.dot(n, wu_ref[...], preferred_element_type=F32)
        h = jnp.square(jnp.maximum(h, 0.0)).astype(BF16)
        return jnp.dot(h, wd_ref[...], preferred_element_type=F32)

    @pl.when(j == 0)
    def _():
        for r0 in range(0, tm, ROW_CHUNK):
            x = x_ref[r0:r0 + ROW_CHUNK, :]
            ms = jnp.mean(x * x, axis=-1, keepdims=True)
            n = (x * lax.rsqrt(ms + NORM_EPS) * nw_ref[...]).astype(BF16)
            n_ref[r0:r0 + ROW_CHUNK, :] = n
            o_ref[r0:r0 + ROW_CHUNK, :] = x + ff(n)

    @pl.when(j > 0)
    def _():
        o_ref[...] += ff(n_ref[...])


def _mlp(x2, norm_w, w_up, w_down, layer, tm, tf):
    m, d = x2.shape
    f = w_up.shape[2]
    return pl.pallas_call(
        _mlp_kernel,
        grid=(m // tm, f // tf),
        in_specs=[
            pl.BlockSpec((tm, d), lambda i, j: (i, 0)),
            pl.BlockSpec((1, d), lambda i, j: (0, 0)),
            pl.BlockSpec((None, d, tf), lambda i, j: (layer, 0, j)),
            pl.BlockSpec((None, tf, d), lambda i, j: (layer, j, 0)),
        ],
        out_specs=pl.BlockSpec((tm, d), lambda i, j: (i, 0)),
        out_shape=jax.ShapeDtypeStruct((m, d), F32),
        scratch_shapes=[pltpu.VMEM((tm, d), BF16)],
        compiler_params=_cparams(("parallel", "arbitrary")),
        name="mlp",
    )(x2, norm_w, w_up, w_down)


def _lambda_init(layer_idx):
    return 0.8 - 0.6 * math.exp(-0.3 * layer_idx)


def kernel(x, positions, attn_norm_w, w_in, gla_gate_w2, gla_gate_b, gla_out_norm_w,
           diff_q_norm_w, diff_k_norm_w, diff_lambda_q1, diff_lambda_k1, diff_lambda_q2,
           diff_lambda_k2, diff_subln_w, w_out, mlp_norm_w, w_up, w_down):
    batch, seq, d = x.shape
    depth = w_in.shape[0]
    m = batch * seq
    assert d == GLA_WIDTH + DIFF_WIDTH and seq % GLA_CHUNK == 0 and seq % Q_TILE == 0
    tm_proj = min(1024, m)
    tm_out = min(512, m)
    tm_mlp = min(512, m)

    half = DIFF_HEAD_DIM // 2
    inv_freq = ROPE_THETA ** (-jnp.arange(0, DIFF_HEAD_DIM, 2, dtype=F32) / DIFF_HEAD_DIM)
    ang = (positions.astype(F32)[..., None] * inv_freq).reshape(m, half)
    cos, sin = jnp.cos(ang), jnp.sin(ang)

    wstack_np, level_np = _gla_constants()
    wstack = jnp.asarray(wstack_np, dtype=BF16)
    level = jnp.asarray(level_np)

    sizes = (GLA_KEY_WIDTH, GLA_KEY_WIDTH, GLA_WIDTH, GLA_WIDTH, GLA_GATE_RANK,
             DIFF_WIDTH, DIFF_WIDTH, DIFF_WIDTH)
    offs = np.concatenate([[0], np.cumsum(sizes)])
    lr0, lr1 = int(offs[4]), int(offs[5])

    w_a = w_in[:, :, :lr0].astype(BF16)
    w_b = w_in[:, :, lr1:].astype(BF16)
    w_lr = jnp.pad(w_in[:, :, lr0:lr1],
                   ((0, 0), (0, 0), (0, V7X_LANES - GLA_GATE_RANK))).astype(BF16)
    w_out_b = w_out.astype(BF16)
    w_up_b = w_up.astype(BF16)
    w_down_b = w_down.astype(BF16)

    x2 = x.reshape(m, d)
    for layer in range(depth):
        w2p = jnp.pad(gla_gate_w2[layer], ((0, V7X_LANES - GLA_GATE_RANK), (0, 0))).astype(BF16)

        proj, lr = _in_proj(x2, attn_norm_w[layer][None, :], w_a, w_b, w_lr, layer, tm_proj, 1024)
        y_gla = _gla(proj, lr, w2p, gla_gate_b[layer][None, :], wstack, level,
                     gla_out_norm_w[layer][None, :], batch, seq)
        y_diff = _diff_attn(proj, cos, sin,
                            diff_q_norm_w[layer][None, :], diff_k_norm_w[layer][None, :],
                            jnp.roll(diff_q_norm_w[layer], DIFF_HEAD_DIM // 2)[None, :],
                            jnp.roll(diff_k_norm_w[layer], DIFF_HEAD_DIM // 2)[None, :],
                            diff_lambda_q1[layer][None, :], diff_lambda_k1[layer][None, :],
                            diff_lambda_q2[layer][None, :], diff_lambda_k2[layer][None, :],
                            diff_subln_w[layer][None, :], batch, seq, _lambda_init(layer))
        x2 = _out_proj(x2, y_gla, y_diff, w_out_b, layer, tm_out)
        x2 = _mlp(x2, mlp_norm_w[layer][None, :], w_up_b, w_down_b, layer, tm_mlp, 1024)
    return x2.reshape(batch, seq, d)
```

```python
import functools
import math

import numpy as np
import jax
import jax.numpy as jnp
from jax import lax
from jax.experimental import pallas as pl
from jax.experimental.pallas import tpu as pltpu

F32 = jnp.float32
BF16 = jnp.bfloat16

GLA_HEADS = 4
GLA_HEAD_K = 128
GLA_HEAD_V = 256
GLA_KEY_WIDTH = GLA_HEADS * GLA_HEAD_K
GLA_WIDTH = GLA_HEADS * GLA_HEAD_V
GLA_GATE_RANK = 16
GLA_GATE_TAU = 16.0
DIFF_HEADS = 4
DIFF_HEAD_DIM = 128
DIFF_WIDTH = DIFF_HEADS * 2 * DIFF_HEAD_DIM
ROPE_THETA = 10000.0
NORM_EPS = 1e-6
SUBLN_EPS = 1e-5
LOG2E = math.log2(math.e)

V7X_LANES = 128
V7X_VMEM_BYTES = 64 * 1024 * 1024
VMEM_LIMIT_BYTES = 56 * 1024 * 1024

COL_GQ, COL_GK, COL_GV, COL_GG = 0, 512, 1024, 2048
COL_DQ, COL_DK, COL_DV = 3072, 4096, 5120
MAIN_COLS = 6144

GLA_CHUNK = 256
GLA_LEVELS = int(math.log2(GLA_CHUNK))
GLA_FAST_MAX_DECAY = 64.0
Q_TILE = 256
ROW_CHUNK = 256
SCORE_BOUND_NO_SHIFT = 60.0


def _cparams(sem):
    return pltpu.CompilerParams(dimension_semantics=sem, vmem_limit_bytes=VMEM_LIMIT_BYTES)


def _in_proj_kernel(x_ref, nw_ref, wa_ref, wb_ref, wlr_ref, o_ref, olr_ref):
    tm = x_ref.shape[0]
    na = wa_ref.shape[1]
    for r0 in range(0, tm, ROW_CHUNK):
        x = x_ref[r0:r0 + ROW_CHUNK, :]
        ms = jnp.mean(x * x, axis=-1, keepdims=True)
        n = (x * lax.rsqrt(ms + NORM_EPS) * nw_ref[...]).astype(BF16)
        o_ref[r0:r0 + ROW_CHUNK, 0:na] = jnp.dot(
            n, wa_ref[...], preferred_element_type=F32).astype(BF16)
        o_ref[r0:r0 + ROW_CHUNK, na:] = jnp.dot(
            n, wb_ref[...], preferred_element_type=F32).astype(BF16)
        olr_ref[r0:r0 + ROW_CHUNK, :] = jnp.dot(
            n, wlr_ref[...], preferred_element_type=F32).astype(BF16)


def _in_proj(x2, norm_w, w_a, w_b, w_lr, layer, tm):
    m, d = x2.shape
    resident = pl.Buffered(1)
    return pl.pallas_call(
        _in_proj_kernel,
        grid=(m // tm,),
        in_specs=[
            pl.BlockSpec((tm, d), lambda i: (i, 0)),
            pl.BlockSpec((1, d), lambda i: (0, 0)),
            pl.BlockSpec((None, d, w_a.shape[2]), lambda i: (layer, 0, 0), pipeline_mode=resident),
            pl.BlockSpec((None, d, w_b.shape[2]), lambda i: (layer, 0, 0), pipeline_mode=resident),
            pl.BlockSpec((None, d, V7X_LANES), lambda i: (layer, 0, 0), pipeline_mode=resident),
        ],
        out_specs=[
            pl.BlockSpec((tm, MAIN_COLS), lambda i: (i, 0)),
            pl.BlockSpec((tm, V7X_LANES), lambda i: (i, 0)),
        ],
        out_shape=[
            jax.ShapeDtypeStruct((m, MAIN_COLS), BF16),
            jax.ShapeDtypeStruct((m, V7X_LANES), BF16),
        ],
        compiler_params=_cparams(("parallel",)),
        name="in_proj",
    )(x2, norm_w, w_a, w_b, w_lr)


def _gla_constants():
    c = GLA_CHUNK
    i = np.arange(c)[:, None]
    t = np.arange(c)[None, :]
    mats = [(t <= i), (t > i)]
    for lvl in range(GLA_LEVELS):
        s = 1 << lvl
        m = (i // (2 * s)) * (2 * s) + s - 1
        mats.append((t > np.minimum(i, m)) & (t <= np.maximum(i, m)))
    wstack = np.concatenate(mats, axis=0).astype(np.float32)
    x = i ^ t
    level = np.where(i > t, np.floor(np.log2(np.maximum(x, 1))), np.where(i == t, -1, -2))
    return wstack, level.astype(np.int32)


def _gla_kernel(q_ref, k_ref, v_ref, g_ref, lr_ref, w2_ref, gb_ref, wstk_ref, lvl_ref, onw_ref,
                o_ref, state_ref, b2_ref, *, seq):
    c = GLA_CHUNK
    dk = GLA_HEAD_K
    n_chunks = seq // c
    w2 = w2_ref[...]
    gb = gb_ref[...]
    onw = onw_ref[...]
    eye = (lax.broadcasted_iota(jnp.int32, (dk, dk), 0)
           == lax.broadcasted_iota(jnp.int32, (dk, dk), 1))
    nt = (((1,), (1,)), ((), ()))
    tn = (((0,), (0,)), ((), ()))
    norm_eps = NORM_EPS * dk

    def log_decay(r0):
        z = jnp.dot(lr_ref[pl.ds(r0, c), :], w2, preferred_element_type=F32) + gb
        la = (jnp.minimum(z, 0.0) - jnp.log1p(jnp.exp(-jnp.abs(z)))) * (1.0 / GLA_GATE_TAU)
        la_hi = la.astype(BF16)
        la_lo = (la - la_hi.astype(F32)).astype(BF16)
        return jnp.concatenate([la_hi, la_lo], axis=1)

    def finish(r0, o, kd, vb, e_last, state):
        g = g_ref[pl.ds(r0, c), :].astype(F32)
        y = o * lax.rsqrt(jnp.mean(o * o, axis=-1, keepdims=True) + norm_eps) * onw
        y = y * (g * jax.nn.sigmoid(g))
        o_ref[pl.ds(r0, c), :] = y.astype(o_ref.dtype)
        e_col = jnp.sum(jnp.where(eye, e_last, 0.0), axis=1, keepdims=True)
        state_ref[...] = state * e_col + lax.dot_general(kd, vb, tn, preferred_element_type=F32)

    def prepass(ci, tot):
        r0 = pl.multiple_of(ci * c, c)
        cs = jnp.dot(wstk_ref[0:c, :], log_decay(r0), preferred_element_type=F32)
        b2 = (cs[:, :dk] + cs[:, dk:]) * LOG2E
        b2_ref[pl.ds(r0, c), :] = b2
        return jnp.maximum(tot, -b2[c - 1:c, :])

    tot = lax.fori_loop(0, n_chunks, prepass, jnp.zeros((1, dk), F32), unroll=4)
    bounded = jnp.max(tot, axis=-1, keepdims=True)[0, 0] <= GLA_FAST_MAX_DECAY
    state_ref[...] = jnp.zeros_like(state_ref)

    def fast_chunk(ci, carry):
        r0 = pl.multiple_of(ci * c, c)
        qf = q_ref[pl.ds(r0, c), :].astype(F32)
        kf = k_ref[pl.ds(r0, c), :].astype(F32)
        vb = v_ref[pl.ds(r0, c), :]
        b2 = b2_ref[pl.ds(r0, c), :]
        eq = jnp.exp2(b2)
        qt = (qf * eq).astype(BF16)
        kt = kf * jnp.exp2(-b2)
        state = state_ref[...]
        causal = (lax.broadcasted_iota(jnp.int32, (c, c), 0)
                  >= lax.broadcasted_iota(jnp.int32, (c, c), 1))
        p = lax.dot_general(qt, kt.astype(BF16), nt, preferred_element_type=F32)
        o = jnp.dot(qt, state.astype(BF16), preferred_element_type=F32)
        o = o + jnp.dot(jnp.where(causal, p, 0.0).astype(BF16), vb, preferred_element_type=F32)
        e_last = eq[c - 1:c, :]
        finish(r0, o, (kt * e_last).astype(BF16), vb, e_last, state)
        return carry

    def level_chunk(ci, carry):
        r0 = pl.multiple_of(ci * c, c)
        lvl = lvl_ref[...]
        row = lax.broadcasted_iota(jnp.int32, (c, 1), 0)
        qf = q_ref[pl.ds(r0, c), :].astype(F32)
        kf = k_ref[pl.ds(r0, c), :].astype(F32)
        kb = k_ref[pl.ds(r0, c), :]
        vb = v_ref[pl.ds(r0, c), :]
        sums = jnp.dot(wstk_ref[...], log_decay(r0), preferred_element_type=F32)
        sums = sums[:, :dk] + sums[:, dk:]
        b = sums[0:c]
        b_rev = sums[c:2 * c]
        state = state_ref[...]
        o = jnp.dot((qf * jnp.exp(b)).astype(BF16), state.astype(BF16),
                    preferred_element_type=F32)
        attn = jnp.where(lvl == -1,
                         lax.dot_general(qf.astype(BF16), kb, nt, preferred_element_type=F32), 0.0)
        for l in range(GLA_LEVELS):
            e = jnp.exp(sums[(l + 2) * c:(l + 3) * c])
            upper = ((row >> l) & 1) == 1
            qt = jnp.where(upper, qf * e, 0.0).astype(BF16)
            kt = jnp.where(upper, 0.0, kf * e).astype(BF16)
            p = lax.dot_general(qt, kt, nt, preferred_element_type=F32)
            attn = jnp.where(lvl == l, p, attn)
        o = o + jnp.dot(attn.astype(BF16), vb, preferred_element_type=F32)
        finish(r0, o, (kf * jnp.exp(b_rev)).astype(BF16), vb, jnp.exp(b[c - 1:c, :]), state)
        return carry

    @pl.when(bounded)
    def _():
        lax.fori_loop(0, n_chunks, fast_chunk, 0, unroll=4)

    @pl.when(jnp.logical_not(bounded))
    def _():
        lax.fori_loop(0, n_chunks, level_chunk, 0)


def _gla(proj, lr, w2p, gate_b, wstack, level, out_norm_w, batch, seq):
    m = proj.shape[0]
    kq = GLA_HEAD_K
    kv = GLA_HEAD_V
    rows = wstack.shape[0]
    return pl.pallas_call(
        functools.partial(_gla_kernel, seq=seq),
        grid=(batch, GLA_HEADS),
        in_specs=[
            pl.BlockSpec((seq, kq), lambda b, h: (b, COL_GQ // kq + h)),
            pl.BlockSpec((seq, kq), lambda b, h: (b, COL_GK // kq + h)),
            pl.BlockSpec((seq, kv), lambda b, h: (b, COL_GV // kv + h)),
            pl.BlockSpec((seq, kv), lambda b, h: (b, COL_GG // kv + h)),
            pl.BlockSpec((seq, V7X_LANES), lambda b, h: (b, 0)),
            pl.BlockSpec((V7X_LANES, kq), lambda b, h: (0, h)),
            pl.BlockSpec((1, kq), lambda b, h: (0, h)),
            pl.BlockSpec((rows, GLA_CHUNK), lambda b, h: (0, 0)),
            pl.BlockSpec((GLA_CHUNK, GLA_CHUNK), lambda b, h: (0, 0)),
            pl.BlockSpec((1, kv), lambda b, h: (0, 0)),
        ],
        out_specs=pl.BlockSpec((seq, kv), lambda b, h: (b, h)),
        out_shape=jax.ShapeDtypeStruct((m, GLA_WIDTH), BF16),
        scratch_shapes=[pltpu.VMEM((kq, kv), F32), pltpu.VMEM((seq, kq), F32)],
        compiler_params=_cparams(("parallel", "parallel")),
        name="gla",
    )(proj, proj, proj, proj, lr, w2p, gate_b, wstack, level, out_norm_w)


def _diff_kernel(q_ref, k_ref, v_ref, cos_ref, sin_ref, qnw_ref, knw_ref, qnr_ref, knr_ref,
                 lq1_ref, lk1_ref, lq2_ref, lk2_ref, sw_ref, o_ref, qs_ref, ks_ref, tab_ref,
                 *, seq, lam_init):
    dh = DIFF_HEAD_DIM
    tq = Q_TILE
    lam = (jnp.exp(jnp.sum(lq1_ref[...] * lk1_ref[...], axis=-1, keepdims=True))
           - jnp.exp(jnp.sum(lq2_ref[...] * lk2_ref[...], axis=-1, keepdims=True))
           + lam_init)
    qscale = (dh ** -0.5) * LOG2E

    @pl.when(pl.program_id(1) == 0)
    def _():
        cosf = jnp.concatenate([cos_ref[...], cos_ref[...]], axis=1)
        sinf = jnp.concatenate([-sin_ref[...], sin_ref[...]], axis=1)
        tab_ref[0] = cosf * (qnw_ref[...] * qscale)
        tab_ref[1] = sinf * (qnr_ref[...] * qscale)
        tab_ref[2] = cosf * knw_ref[...]
        tab_ref[3] = sinf * knr_ref[...]

    perm = (lax.broadcasted_iota(jnp.int32, (dh, dh), 0)
            == ((lax.broadcasted_iota(jnp.int32, (dh, dh), 1) + dh // 2) % dh)).astype(BF16)

    def prep(src_ref, tab, comp, r0, rows):
        tb = src_ref[r0:r0 + rows, comp * dh:(comp + 1) * dh]
        t = tb.astype(F32)
        r = lax.rsqrt(jnp.mean(t * t, axis=-1, keepdims=True) + NORM_EPS)
        tr = jnp.dot(tb, perm, preferred_element_type=F32)
        return (r * (t * tab_ref[tab, r0:r0 + rows, :]
                     + tr * tab_ref[tab + 1, r0:r0 + rows, :])).astype(BF16)

    for comp in range(2):
        qs_ref[comp] = prep(q_ref, 0, comp, 0, seq)
        ks_ref[comp] = prep(k_ref, 2, comp, 0, seq)

    nt = (((1,), (1,)), ((), ()))
    tri = (lax.broadcasted_iota(jnp.int32, (tq, tq), 0)
           >= lax.broadcasted_iota(jnp.int32, (tq, tq), 1))
    sw = sw_ref[...] * (1.0 - lam_init)

    def attend(subtract_max):
        for t in reversed(range(seq // tq)):
            r0 = t * tq
            ps, ls = [], []
            for comp in range(2):
                q = qs_ref[comp, r0:r0 + tq, :]
                s = lax.dot_general(q, ks_ref[comp, r0:r0 + tq, :], nt,
                                    preferred_element_type=F32)
                s = jnp.where(tri, s, -jnp.inf)
                if t > 0:
                    s_off = lax.dot_general(q, ks_ref[comp, 0:r0, :], nt,
                                            preferred_element_type=F32)
                    s = jnp.concatenate([s_off, s], axis=1)
                if subtract_max:
                    s = s - jnp.max(s, axis=-1, keepdims=True)
                p = jnp.exp2(s)
                ls.append(jnp.sum(p, axis=-1, keepdims=True))
                ps.append(p.astype(BF16))
            ratio = (lam * ls[0] / ls[1]).astype(BF16)
            w = ps[0] - ratio * ps[1]
            o = jnp.dot(w, v_ref[0:r0 + tq, :], preferred_element_type=F32) * (1.0 / ls[0])
            o = o * lax.rsqrt(jnp.mean(o * o, axis=-1, keepdims=True) + SUBLN_EPS) * sw
            o_ref[r0:r0 + tq, :] = o.astype(o_ref.dtype)

    bound = (dh * qscale) * (jnp.max(jnp.abs(qnw_ref[...]), axis=-1, keepdims=True)
                             * jnp.max(jnp.abs(knw_ref[...]), axis=-1, keepdims=True))
    small = bound[0, 0] <= SCORE_BOUND_NO_SHIFT

    @pl.when(small)
    def _():
        attend(False)

    @pl.when(jnp.logical_not(small))
    def _():
        attend(True)


def _diff_attn(proj, cos, sin, qnw, knw, qnr, knr, lq1, lk1, lq2, lk2, subln_w, batch, seq,
               lam_init):
    m = proj.shape[0]
    hw = 2 * DIFF_HEAD_DIM
    vec = pl.BlockSpec((1, DIFF_HEAD_DIM), lambda b, h: (0, 0))
    return pl.pallas_call(
        functools.partial(_diff_kernel, seq=seq, lam_init=lam_init),
        grid=(batch, DIFF_HEADS),
        in_specs=[
            pl.BlockSpec((seq, hw), lambda b, h: (b, COL_DQ // hw + h)),
            pl.BlockSpec((seq, hw), lambda b, h: (b, COL_DK // hw + h)),
            pl.BlockSpec((seq, hw), lambda b, h: (b, COL_DV // hw + h)),
            pl.BlockSpec((seq, DIFF_HEAD_DIM // 2), lambda b, h: (b, 0)),
            pl.BlockSpec((seq, DIFF_HEAD_DIM // 2), lambda b, h: (b, 0)),
            vec, vec, vec, vec, vec, vec, vec, vec,
            pl.BlockSpec((1, hw), lambda b, h: (0, 0)),
        ],
        out_specs=pl.BlockSpec((seq, hw), lambda b, h: (b, h)),
        out_shape=jax.ShapeDtypeStruct((m, DIFF_WIDTH), BF16),
        scratch_shapes=[pltpu.VMEM((2, seq, DIFF_HEAD_DIM), BF16),
                        pltpu.VMEM((2, seq, DIFF_HEAD_DIM), BF16),
                        pltpu.VMEM((4, seq, DIFF_HEAD_DIM), F32)],
        compiler_params=_cparams(("parallel", "arbitrary")),
        name="diff_attn",
    )(proj, proj, proj, cos, sin, qnw, knw, qnr, knr, lq1, lk1, lq2, lk2, subln_w)


def _out_proj_kernel(x_ref, yg_ref, yd_ref, wg_ref, wd_ref, o_ref):
    acc = jnp.dot(yg_ref[...], wg_ref[...], preferred_element_type=F32)
    acc = acc + jnp.dot(yd_ref[...], wd_ref[...], preferred_element_type=F32)
    o_ref[...] = x_ref[...] + acc


def _out_proj(x2, yg, yd, w_out, layer, tm):
    m, d = x2.shape
    return pl.pallas_call(
        _out_proj_kernel,
        grid=(m // tm,),
        in_specs=[
            pl.BlockSpec((tm, d), lambda i: (i, 0)),
            pl.BlockSpec((tm, GLA_WIDTH), lambda i: (i, 0)),
            pl.BlockSpec((tm, DIFF_WIDTH), lambda i: (i, 0)),
            pl.BlockSpec((None, GLA_WIDTH, d), lambda i: (layer, 0, 0)),
            pl.BlockSpec((None, DIFF_WIDTH, d), lambda i: (layer, GLA_WIDTH // DIFF_WIDTH, 0)),
        ],
        out_specs=pl.BlockSpec((tm, d), lambda i: (i, 0)),
        out_shape=jax.ShapeDtypeStruct((m, d), F32),
        compiler_params=_cparams(("parallel",)),
        name="out_proj",
    )(x2, yg, yd, w_out, w_out)


def _mlp_kernel(x_ref, nw_ref, wu_ref, wd_ref, o_ref, n_ref):
    j = pl.program_id(1)
    tm = x_ref.shape[0]

    def ff(n):
        h = jnp.dot(n, wu_ref[...], preferred_element_type=F32)
        h = jnp.square(jnp.maximum(h, 0.0)).astype(BF16)
        return jnp.dot(h, wd_ref[...], preferred_element_type=F32)

    @pl.when(j == 0)
    def _():
        for r0 in range(0, tm, ROW_CHUNK):
            x = x_ref[r0:r0 + ROW_CHUNK, :]
            ms = jnp.mean(x * x, axis=-1, keepdims=True)
            n = (x * lax.rsqrt(ms + NORM_EPS) * nw_ref[...]).astype(BF16)
            n_ref[r0:r0 + ROW_CHUNK, :] = n
            o_ref[r0:r0 + ROW_CHUNK, :] = x + ff(n)

    @pl.when(j > 0)
    def _():
        o_ref[...] += ff(n_ref[...])


def _mlp(x2, norm_w, w_up, w_down, layer, tm, tf):
    m, d = x2.shape
    f = w_up.shape[2]
    return pl.pallas_call(
        _mlp_kernel,
        grid=(m // tm, f // tf),
        in_specs=[
            pl.BlockSpec((tm, d), lambda i, j: (i, 0)),
            pl.BlockSpec((1, d), lambda i, j: (0, 0)),
            pl.BlockSpec((None, d, tf), lambda i, j: (layer, 0, j)),
            pl.BlockSpec((None, tf, d), lambda i, j: (layer, j, 0)),
        ],
        out_specs=pl.BlockSpec((tm, d), lambda i, j: (i, 0)),
        out_shape=jax.ShapeDtypeStruct((m, d), F32),
        scratch_shapes=[pltpu.VMEM((tm, d), BF16)],
        compiler_params=_cparams(("parallel", "arbitrary")),
        name="mlp",
    )(x2, norm_w, w_up, w_down)


def _lambda_init(layer_idx):
    return 0.8 - 0.6 * math.exp(-0.3 * layer_idx)


def kernel(x, positions, attn_norm_w, w_in, gla_gate_w2, gla_gate_b, gla_out_norm_w,
           diff_q_norm_w, diff_k_norm_w, diff_lambda_q1, diff_lambda_k1, diff_lambda_q2,
           diff_lambda_k2, diff_subln_w, w_out, mlp_norm_w, w_up, w_down):
    batch, seq, d = x.shape
    depth = w_in.shape[0]
    m = batch * seq
    assert d == GLA_WIDTH + DIFF_WIDTH and seq % GLA_CHUNK == 0 and seq % Q_TILE == 0
    tm_proj = min(512, m)
    tm_out = min(512, m)
    tm_mlp = min(512, m)

    half = DIFF_HEAD_DIM // 2
    inv_freq = ROPE_THETA ** (-jnp.arange(0, DIFF_HEAD_DIM, 2, dtype=F32) / DIFF_HEAD_DIM)
    ang = (positions.astype(F32)[..., None] * inv_freq).reshape(m, half)
    cos, sin = jnp.cos(ang), jnp.sin(ang)

    wstack_np, level_np = _gla_constants()
    wstack = jnp.asarray(wstack_np, dtype=BF16)
    level = jnp.asarray(level_np)

    sizes = (GLA_KEY_WIDTH, GLA_KEY_WIDTH, GLA_WIDTH, GLA_WIDTH, GLA_GATE_RANK,
             DIFF_WIDTH, DIFF_WIDTH, DIFF_WIDTH)
    offs = np.concatenate([[0], np.cumsum(sizes)])
    lr0, lr1 = int(offs[4]), int(offs[5])

    w_a = w_in[:, :, :lr0].astype(BF16)
    w_b = w_in[:, :, lr1:].astype(BF16)
    w_lr = jnp.pad(w_in[:, :, lr0:lr1],
                   ((0, 0), (0, 0), (0, V7X_LANES - GLA_GATE_RANK))).astype(BF16)
    w_out_b = w_out.astype(BF16)
    w_up_b = w_up.astype(BF16)
    w_down_b = w_down.astype(BF16)

    x2 = x.reshape(m, d)
    for layer in range(depth):
        w2p = jnp.pad(gla_gate_w2[layer], ((0, V7X_LANES - GLA_GATE_RANK), (0, 0))).astype(BF16)

        proj, lr = _in_proj(x2, attn_norm_w[layer][None, :], w_a, w_b, w_lr, layer, tm_proj)
        y_gla = _gla(proj, lr, w2p, gla_gate_b[layer][None, :], wstack, level,
                     gla_out_norm_w[layer][None, :], batch, seq)
        y_diff = _diff_attn(proj, cos, sin,
                            diff_q_norm_w[layer][None, :], diff_k_norm_w[layer][None, :],
                            jnp.roll(diff_q_norm_w[layer], DIFF_HEAD_DIM // 2)[None, :],
                            jnp.roll(diff_k_norm_w[layer], DIFF_HEAD_DIM // 2)[None, :],
                            diff_lambda_q1[layer][None, :], diff_lambda_k1[layer][None, :],
                            diff_lambda_q2[layer][None, :], diff_lambda_k2[layer][None, :],
                            diff_subln_w[layer][None, :], batch, seq, _lambda_init(layer))
        x2 = _out_proj(x2, y_gla, y_diff, w_out_b, layer, tm_out)
        x2 = _mlp(x2, mlp_norm_w[layer][None, :], w_up_b, w_down_b, layer, tm_mlp, 1024)
    return x2.reshape(batch, seq, d)
```

```python
import functools
import math

import numpy as np
import jax
import jax.numpy as jnp
from jax import lax
from jax.experimental import pallas as pl
from jax.experimental.pallas import tpu as pltpu

F32 = jnp.float32
BF16 = jnp.bfloat16

GLA_HEADS = 4
GLA_HEAD_K = 128
GLA_HEAD_V = 256
GLA_KEY_WIDTH = GLA_HEADS * GLA_HEAD_K
GLA_WIDTH = GLA_HEADS * GLA_HEAD_V
GLA_GATE_RANK = 16
GLA_GATE_TAU = 16.0
DIFF_HEADS = 4
DIFF_HEAD_DIM = 128
DIFF_WIDTH = DIFF_HEADS * 2 * DIFF_HEAD_DIM
ROPE_THETA = 10000.0
NORM_EPS = 1e-6
SUBLN_EPS = 1e-5
LOG2E = math.log2(math.e)
DIFF_QSCALE = (DIFF_HEAD_DIM ** -0.5) * LOG2E

V7X_LANES = 128
V7X_VMEM_BYTES = 64 * 1024 * 1024
VMEM_LIMIT_BYTES = 56 * 1024 * 1024

COL_GQ, COL_GK, COL_GV, COL_GG = 0, 512, 1024, 2048
COL_DQ, COL_DK, COL_DV = 3072, 4096, 5120
MAIN_COLS = 6144

GLA_CHUNK = 256
GLA_LEVELS = int(math.log2(GLA_CHUNK))
GLA_FAST_MAX_DECAY = 64.0
Q_TILE = 256
ROW_CHUNK = 256
SCORE_BOUND_NO_SHIFT = 60.0


def _cparams(sem):
    return pltpu.CompilerParams(dimension_semantics=sem, vmem_limit_bytes=VMEM_LIMIT_BYTES)


def _in_proj_kernel(x_ref, nw_ref, wa_ref, wb_ref, wlr_ref, cos_ref, sin_ref,
                    qnw_ref, qnr_ref, knw_ref, knr_ref, o_ref, olr_ref):
    tm = x_ref.shape[0]
    na = wa_ref.shape[1]
    dh = DIFF_HEAD_DIM
    n_half_heads = DIFF_WIDTH // dh
    for r0 in range(0, tm, ROW_CHUNK):
        rows = slice(r0, r0 + ROW_CHUNK)
        x = x_ref[rows, :]
        ms = jnp.mean(x * x, axis=-1, keepdims=True)
        n = (x * lax.rsqrt(ms + NORM_EPS) * nw_ref[...]).astype(BF16)
        o_ref[rows, 0:na] = jnp.dot(n, wa_ref[...], preferred_element_type=F32).astype(BF16)
        olr_ref[rows, :] = jnp.dot(n, wlr_ref[...], preferred_element_type=F32).astype(BF16)
        ob = jnp.dot(n, wb_ref[...], preferred_element_type=F32)

        cosf = jnp.concatenate([cos_ref[rows, :], cos_ref[rows, :]], axis=1)
        sinf = jnp.concatenate([-sin_ref[rows, :], sin_ref[rows, :]], axis=1)
        tabs = ((cosf * (qnw_ref[...] * DIFF_QSCALE), sinf * (qnr_ref[...] * DIFF_QSCALE)),
                (cosf * knw_ref[...], sinf * knr_ref[...]))
        for i in range(2 * n_half_heads):
            ta, tb = tabs[i // n_half_heads]
            t = ob[:, i * dh:(i + 1) * dh]
            r = lax.rsqrt(jnp.mean(t * t, axis=-1, keepdims=True) + NORM_EPS)
            o_ref[rows, na + i * dh:na + (i + 1) * dh] = (
                r * (t * ta + pltpu.roll(t, dh // 2, 1) * tb)).astype(BF16)
        o_ref[rows, na + 2 * DIFF_WIDTH:] = ob[:, 2 * DIFF_WIDTH:].astype(BF16)


def _in_proj(x2, norm_w, w_a, w_b, w_lr, cos, sin, qnw, qnr, knw, knr, layer, tm):
    m, d = x2.shape
    resident = pl.Buffered(1)
    half = DIFF_HEAD_DIM // 2
    vec = pl.BlockSpec((1, DIFF_HEAD_DIM), lambda i: (0, 0))
    return pl.pallas_call(
        _in_proj_kernel,
        grid=(m // tm,),
        in_specs=[
            pl.BlockSpec((tm, d), lambda i: (i, 0)),
            pl.BlockSpec((1, d), lambda i: (0, 0)),
            pl.BlockSpec((None, d, w_a.shape[2]), lambda i: (layer, 0, 0), pipeline_mode=resident),
            pl.BlockSpec((None, d, w_b.shape[2]), lambda i: (layer, 0, 0), pipeline_mode=resident),
            pl.BlockSpec((None, d, V7X_LANES), lambda i: (layer, 0, 0), pipeline_mode=resident),
            pl.BlockSpec((tm, half), lambda i: (i, 0)),
            pl.BlockSpec((tm, half), lambda i: (i, 0)),
            vec, vec, vec, vec,
        ],
        out_specs=[
            pl.BlockSpec((tm, MAIN_COLS), lambda i: (i, 0)),
            pl.BlockSpec((tm, V7X_LANES), lambda i: (i, 0)),
        ],
        out_shape=[
            jax.ShapeDtypeStruct((m, MAIN_COLS), BF16),
            jax.ShapeDtypeStruct((m, V7X_LANES), BF16),
        ],
        compiler_params=_cparams(("parallel",)),
        name="in_proj",
    )(x2, norm_w, w_a, w_b, w_lr, cos, sin, qnw, qnr, knw, knr)


def _gla_constants():
    c = GLA_CHUNK
    i = np.arange(c)[:, None]
    t = np.arange(c)[None, :]
    mats = [(t <= i), (t > i)]
    for lvl in range(GLA_LEVELS):
        s = 1 << lvl
        m = (i // (2 * s)) * (2 * s) + s - 1
        mats.append((t > np.minimum(i, m)) & (t <= np.maximum(i, m)))
    wstack = np.concatenate(mats, axis=0).astype(np.float32)
    x = i ^ t
    level = np.where(i > t, np.floor(np.log2(np.maximum(x, 1))), np.where(i == t, -1, -2))
    return wstack, level.astype(np.int32)


def _gla_kernel(q_ref, k_ref, v_ref, g_ref, lr_ref, w2_ref, gb_ref, wstk_ref, lvl_ref, onw_ref,
                o_ref, state_ref, b2_ref, *, seq):
    c = GLA_CHUNK
    dk = GLA_HEAD_K
    n_chunks = seq // c
    w2 = w2_ref[...]
    gb = gb_ref[...]
    onw = onw_ref[...]
    eye = (lax.broadcasted_iota(jnp.int32, (dk, dk), 0)
           == lax.broadcasted_iota(jnp.int32, (dk, dk), 1))
    nt = (((1,), (1,)), ((), ()))
    tn = (((0,), (0,)), ((), ()))
    norm_eps = NORM_EPS * dk

    def log_decay(r0):
        z = jnp.dot(lr_ref[pl.ds(r0, c), :], w2, preferred_element_type=F32) + gb
        la = (jnp.minimum(z, 0.0) - jnp.log1p(jnp.exp(-jnp.abs(z)))) * (1.0 / GLA_GATE_TAU)
        la_hi = la.astype(BF16)
        la_lo = (la - la_hi.astype(F32)).astype(BF16)
        return jnp.concatenate([la_hi, la_lo], axis=1)

    def finish(r0, o, kd, vb, e_last, state):
        g = g_ref[pl.ds(r0, c), :].astype(F32)
        y = o * lax.rsqrt(jnp.mean(o * o, axis=-1, keepdims=True) + norm_eps) * onw
        y = y * (g * jax.nn.sigmoid(g))
        o_ref[pl.ds(r0, c), :] = y.astype(o_ref.dtype)
        e_col = jnp.sum(jnp.where(eye, e_last, 0.0), axis=1, keepdims=True)
        state_ref[...] = state * e_col + lax.dot_general(kd, vb, tn, preferred_element_type=F32)

    def prepass(ci, tot):
        r0 = pl.multiple_of(ci * c, c)
        cs = jnp.dot(wstk_ref[0:c, :], log_decay(r0), preferred_element_type=F32)
        b2 = (cs[:, :dk] + cs[:, dk:]) * LOG2E
        b2_ref[pl.ds(r0, c), :] = b2
        return jnp.maximum(tot, -b2[c - 1:c, :])

    tot = lax.fori_loop(0, n_chunks, prepass, jnp.zeros((1, dk), F32), unroll=4)
    bounded = jnp.max(tot, axis=-1, keepdims=True)[0, 0] <= GLA_FAST_MAX_DECAY
    state_ref[...] = jnp.zeros_like(state_ref)

    def fast_chunk(ci, carry):
        r0 = pl.multiple_of(ci * c, c)
        qf = q_ref[pl.ds(r0, c), :].astype(F32)
        kf = k_ref[pl.ds(r0, c), :].astype(F32)
        vb = v_ref[pl.ds(r0, c), :]
        b2 = b2_ref[pl.ds(r0, c), :]
        eq = jnp.exp2(b2)
        qt = (qf * eq).astype(BF16)
        kt = kf * jnp.exp2(-b2)
        state = state_ref[...]
        causal = (lax.broadcasted_iota(jnp.int32, (c, c), 0)
                  >= lax.broadcasted_iota(jnp.int32, (c, c), 1))
        p = lax.dot_general(qt, kt.astype(BF16), nt, preferred_element_type=F32)
        o = jnp.dot(qt, state.astype(BF16), preferred_element_type=F32)
        o = o + jnp.dot(jnp.where(causal, p, 0.0).astype(BF16), vb, preferred_element_type=F32)
        e_last = eq[c - 1:c, :]
        finish(r0, o, (kt * e_last).astype(BF16), vb, e_last, state)
        return carry

    def level_chunk(ci, carry):
        r0 = pl.multiple_of(ci * c, c)
        lvl = lvl_ref[...]
        row = lax.broadcasted_iota(jnp.int32, (c, 1), 0)
        qf = q_ref[pl.ds(r0, c), :].astype(F32)
        kf = k_ref[pl.ds(r0, c), :].astype(F32)
        kb = k_ref[pl.ds(r0, c), :]
        vb = v_ref[pl.ds(r0, c), :]
        sums = jnp.dot(wstk_ref[...], log_decay(r0), preferred_element_type=F32)
        sums = sums[:, :dk] + sums[:, dk:]
        b = sums[0:c]
        b_rev = sums[c:2 * c]
        state = state_ref[...]
        o = jnp.dot((qf * jnp.exp(b)).astype(BF16), state.astype(BF16),
                    preferred_element_type=F32)
        attn = jnp.where(lvl == -1,
                         lax.dot_general(qf.astype(BF16), kb, nt, preferred_element_type=F32), 0.0)
        for l in range(GLA_LEVELS):
            e = jnp.exp(sums[(l + 2) * c:(l + 3) * c])
            upper = ((row >> l) & 1) == 1
            qt = jnp.where(upper, qf * e, 0.0).astype(BF16)
            kt = jnp.where(upper, 0.0, kf * e).astype(BF16)
            p = lax.dot_general(qt, kt, nt, preferred_element_type=F32)
            attn = jnp.where(lvl == l, p, attn)
        o = o + jnp.dot(attn.astype(BF16), vb, preferred_element_type=F32)
        finish(r0, o, (kf * jnp.exp(b_rev)).astype(BF16), vb, jnp.exp(b[c - 1:c, :]), state)
        return carry

    @pl.when(bounded)
    def _():
        lax.fori_loop(0, n_chunks, fast_chunk, 0, unroll=4)

    @pl.when(jnp.logical_not(bounded))
    def _():
        lax.fori_loop(0, n_chunks, level_chunk, 0)


def _gla(proj, lr, w2p, gate_b, wstack, level, out_norm_w, batch, seq):
    m = proj.shape[0]
    kq = GLA_HEAD_K
    kv = GLA_HEAD_V
    rows = wstack.shape[0]
    return pl.pallas_call(
        functools.partial(_gla_kernel, seq=seq),
        grid=(batch, GLA_HEADS),
        in_specs=[
            pl.BlockSpec((seq, kq), lambda b, h: (b, COL_GQ // kq + h)),
            pl.BlockSpec((seq, kq), lambda b, h: (b, COL_GK // kq + h)),
            pl.BlockSpec((seq, kv), lambda b, h: (b, COL_GV // kv + h)),
            pl.BlockSpec((seq, kv), lambda b, h: (b, COL_GG // kv + h)),
            pl.BlockSpec((seq, V7X_LANES), lambda b, h: (b, 0)),
            pl.BlockSpec((V7X_LANES, kq), lambda b, h: (0, h)),
            pl.BlockSpec((1, kq), lambda b, h: (0, h)),
            pl.BlockSpec((rows, GLA_CHUNK), lambda b, h: (0, 0)),
            pl.BlockSpec((GLA_CHUNK, GLA_CHUNK), lambda b, h: (0, 0)),
            pl.BlockSpec((1, kv), lambda b, h: (0, 0)),
        ],
        out_specs=pl.BlockSpec((seq, kv), lambda b, h: (b, h)),
        out_shape=jax.ShapeDtypeStruct((m, GLA_WIDTH), BF16),
        scratch_shapes=[pltpu.VMEM((kq, kv), F32), pltpu.VMEM((seq, kq), F32)],
        compiler_params=_cparams(("parallel", "parallel")),
        name="gla",
    )(proj, proj, proj, proj, lr, w2p, gate_b, wstack, level, out_norm_w)


def _diff_kernel(q_ref, k_ref, v_ref, qnw_ref, knw_ref, lq1_ref, lk1_ref, lq2_ref, lk2_ref,
                 sw_ref, o_ref, *, seq, lam_init):
    dh = DIFF_HEAD_DIM
    tq = Q_TILE
    lam = (jnp.exp(jnp.sum(lq1_ref[...] * lk1_ref[...], axis=-1, keepdims=True))
           - jnp.exp(jnp.sum(lq2_ref[...] * lk2_ref[...], axis=-1, keepdims=True))
           + lam_init)

    nt = (((1,), (1,)), ((), ()))
    tri = (lax.broadcasted_iota(jnp.int32, (tq, tq), 0)
           >= lax.broadcasted_iota(jnp.int32, (tq, tq), 1))
    sw = sw_ref[...] * (1.0 - lam_init)

    def attend(subtract_max):
        for t in reversed(range(seq // tq)):
            r0 = t * tq
            ps, ls = [], []
            for comp in range(2):
                cols = slice(comp * dh, (comp + 1) * dh)
                q = q_ref[r0:r0 + tq, cols]
                s = lax.dot_general(q, k_ref[r0:r0 + tq, cols], nt, preferred_element_type=F32)
                s = jnp.where(tri, s, -jnp.inf)
                if t > 0:
                    s_off = lax.dot_general(q, k_ref[0:r0, cols], nt,
                                            preferred_element_type=F32)
                    s = jnp.concatenate([s_off, s], axis=1)
                if subtract_max:
                    s = s - jnp.max(s, axis=-1, keepdims=True)
                p = jnp.exp2(s)
                ls.append(jnp.sum(p, axis=-1, keepdims=True))
                ps.append(p.astype(BF16))
            ratio = (lam * ls[0] / ls[1]).astype(BF16)
            w = ps[0] - ratio * ps[1]
            o = jnp.dot(w, v_ref[0:r0 + tq, :], preferred_element_type=F32) * (1.0 / ls[0])
            o = o * lax.rsqrt(jnp.mean(o * o, axis=-1, keepdims=True) + SUBLN_EPS) * sw
            o_ref[r0:r0 + tq, :] = o.astype(o_ref.dtype)

    bound = (dh * DIFF_QSCALE) * (jnp.max(jnp.abs(qnw_ref[...]), axis=-1, keepdims=True)
                             * jnp.max(jnp.abs(knw_ref[...]), axis=-1, keepdims=True))
    small = bound[0, 0] <= SCORE_BOUND_NO_SHIFT

    @pl.when(small)
    def _():
        attend(False)

    @pl.when(jnp.logical_not(small))
    def _():
        attend(True)


def _diff_attn(proj, qnw, knw, lq1, lk1, lq2, lk2, subln_w, batch, seq, lam_init):
    m = proj.shape[0]
    hw = 2 * DIFF_HEAD_DIM
    vec = pl.BlockSpec((1, DIFF_HEAD_DIM), lambda b, h: (0, 0))
    return pl.pallas_call(
        functools.partial(_diff_kernel, seq=seq, lam_init=lam_init),
        grid=(batch, DIFF_HEADS),
        in_specs=[
            pl.BlockSpec((seq, hw), lambda b, h: (b, COL_DQ // hw + h)),
            pl.BlockSpec((seq, hw), lambda b, h: (b, COL_DK // hw + h)),
            pl.BlockSpec((seq, hw), lambda b, h: (b, COL_DV // hw + h)),
            vec, vec, vec, vec, vec, vec,
            pl.BlockSpec((1, hw), lambda b, h: (0, 0)),
        ],
        out_specs=pl.BlockSpec((seq, hw), lambda b, h: (b, h)),
        out_shape=jax.ShapeDtypeStruct((m, DIFF_WIDTH), BF16),
        compiler_params=_cparams(("parallel", "parallel")),
        name="diff_attn",
    )(proj, proj, proj, qnw, knw, lq1, lk1, lq2, lk2, subln_w)


def _out_proj_kernel(x_ref, yg_ref, yd_ref, wg_ref, wd_ref, o_ref):
    acc = jnp.dot(yg_ref[...], wg_ref[...], preferred_element_type=F32)
    acc = acc + jnp.dot(yd_ref[...], wd_ref[...], preferred_element_type=F32)
    o_ref[...] = x_ref[...] + acc


def _out_proj(x2, yg, yd, w_out, layer, tm):
    m, d = x2.shape
    return pl.pallas_call(
        _out_proj_kernel,
        grid=(m // tm,),
        in_specs=[
            pl.BlockSpec((tm, d), lambda i: (i, 0)),
            pl.BlockSpec((tm, GLA_WIDTH), lambda i: (i, 0)),
            pl.BlockSpec((tm, DIFF_WIDTH), lambda i: (i, 0)),
            pl.BlockSpec((None, GLA_WIDTH, d), lambda i: (layer, 0, 0)),
            pl.BlockSpec((None, DIFF_WIDTH, d), lambda i: (layer, GLA_WIDTH // DIFF_WIDTH, 0)),
        ],
        out_specs=pl.BlockSpec((tm, d), lambda i: (i, 0)),
        out_shape=jax.ShapeDtypeStruct((m, d), F32),
        compiler_params=_cparams(("parallel",)),
        name="out_proj",
    )(x2, yg, yd, w_out, w_out)


def _mlp_kernel(x_ref, nw_ref, wu_ref, wd_ref, o_ref, n_ref):
    j = pl.program_id(1)
    tm = x_ref.shape[0]

    def ff(n):
        h = jnp.dot(n, wu_ref[...], preferred_element_type=F32)
        h = jnp.square(jnp.maximum(h, 0.0)).astype(BF16)
        return jnp.dot(h, wd_ref[...], preferred_element_type=F32)

    @pl.when(j == 0)
    def _():
        for r0 in range(0, tm, ROW_CHUNK):
            x = x_ref[r0:r0 + ROW_CHUNK, :]
            ms = jnp.mean(x * x, axis=-1, keepdims=True)
            n = (x * lax.rsqrt(ms + NORM_EPS) * nw_ref[...]).astype(BF16)
            n_ref[r0:r0 + ROW_CHUNK, :] = n
            o_ref[r0:r0 + ROW_CHUNK, :] = x + ff(n)

    @pl.when(j > 0)
    def _():
        o_ref[...] += ff(n_ref[...])


def _mlp(x2, norm_w, w_up, w_down, layer, tm, tf):
    m, d = x2.shape
    f = w_up.shape[2]
    return pl.pallas_call(
        _mlp_kernel,
        grid=(m // tm, f // tf),
        in_specs=[
            pl.BlockSpec((tm, d), lambda i, j: (i, 0)),
            pl.BlockSpec((1, d), lambda i, j: (0, 0)),
            pl.BlockSpec((None, d, tf), lambda i, j: (layer, 0, j)),
            pl.BlockSpec((None, tf, d), lambda i, j: (layer, j, 0)),
        ],
        out_specs=pl.BlockSpec((tm, d), lambda i, j: (i, 0)),
        out_shape=jax.ShapeDtypeStruct((m, d), F32),
        scratch_shapes=[pltpu.VMEM((tm, d), BF16)],
        compiler_params=_cparams(("parallel", "arbitrary")),
        name="mlp",
    )(x2, norm_w, w_up, w_down)


def _lambda_init(layer_idx):
    return 0.8 - 0.6 * math.exp(-0.3 * layer_idx)


def kernel(x, positions, attn_norm_w, w_in, gla_gate_w2, gla_gate_b, gla_out_norm_w,
           diff_q_norm_w, diff_k_norm_w, diff_lambda_q1, diff_lambda_k1, diff_lambda_q2,
           diff_lambda_k2, diff_subln_w, w_out, mlp_norm_w, w_up, w_down):
    batch, seq, d = x.shape
    depth = w_in.shape[0]
    m = batch * seq
    assert d == GLA_WIDTH + DIFF_WIDTH and seq % GLA_CHUNK == 0 and seq % Q_TILE == 0
    tm_proj = min(512, m)
    tm_out = min(512, m)
    tm_mlp = min(512, m)

    half = DIFF_HEAD_DIM // 2
    inv_freq = ROPE_THETA ** (-jnp.arange(0, DIFF_HEAD_DIM, 2, dtype=F32) / DIFF_HEAD_DIM)
    ang = (positions.astype(F32)[..., None] * inv_freq).reshape(m, half)
    cos, sin = jnp.cos(ang), jnp.sin(ang)

    wstack_np, level_np = _gla_constants()
    wstack = jnp.asarray(wstack_np, dtype=BF16)
    level = jnp.asarray(level_np)

    sizes = (GLA_KEY_WIDTH, GLA_KEY_WIDTH, GLA_WIDTH, GLA_WIDTH, GLA_GATE_RANK,
             DIFF_WIDTH, DIFF_WIDTH, DIFF_WIDTH)
    offs = np.concatenate([[0], np.cumsum(sizes)])
    lr0, lr1 = int(offs[4]), int(offs[5])

    w_a = w_in[:, :, :lr0].astype(BF16)
    w_b = w_in[:, :, lr1:].astype(BF16)
    w_lr = jnp.pad(w_in[:, :, lr0:lr1],
                   ((0, 0), (0, 0), (0, V7X_LANES - GLA_GATE_RANK))).astype(BF16)
    w_out_b = w_out.astype(BF16)
    w_up_b = w_up.astype(BF16)
    w_down_b = w_down.astype(BF16)

    x2 = x.reshape(m, d)
    for layer in range(depth):
        w2p = jnp.pad(gla_gate_w2[layer], ((0, V7X_LANES - GLA_GATE_RANK), (0, 0))).astype(BF16)

        qnw = diff_q_norm_w[layer][None, :]
        knw = diff_k_norm_w[layer][None, :]
        proj, lr = _in_proj(x2, attn_norm_w[layer][None, :], w_a, w_b, w_lr, cos, sin,
                            qnw, jnp.roll(qnw, half, axis=1), knw, jnp.roll(knw, half, axis=1),
                            layer, tm_proj)
        y_gla = _gla(proj, lr, w2p, gla_gate_b[layer][None, :], wstack, level,
                     gla_out_norm_w[layer][None, :], batch, seq)
        y_diff = _diff_attn(proj, qnw, knw,
                            diff_lambda_q1[layer][None, :], diff_lambda_k1[layer][None, :],
                            diff_lambda_q2[layer][None, :], diff_lambda_k2[layer][None, :],
                            diff_subln_w[layer][None, :], batch, seq, _lambda_init(layer))
        x2 = _out_proj(x2, y_gla, y_diff, w_out_b, layer, tm_out)
        x2 = _mlp(x2, mlp_norm_w[layer][None, :], w_up_b, w_down_b, layer, tm_mlp, 1024)
    return x2.reshape(batch, seq, d)
```

```python
import functools
import math

import numpy as np
import jax
import jax.numpy as jnp
from jax import lax
from jax.experimental import pallas as pl
from jax.experimental.pallas import tpu as pltpu

F32 = jnp.float32
BF16 = jnp.bfloat16

GLA_HEADS = 4
GLA_HEAD_K = 128
GLA_HEAD_V = 256
GLA_KEY_WIDTH = GLA_HEADS * GLA_HEAD_K
GLA_WIDTH = GLA_HEADS * GLA_HEAD_V
GLA_GATE_RANK = 16
GLA_GATE_TAU = 16.0
DIFF_HEADS = 4
DIFF_HEAD_DIM = 128
DIFF_WIDTH = DIFF_HEADS * 2 * DIFF_HEAD_DIM
ROPE_THETA = 10000.0
NORM_EPS = 1e-6
SUBLN_EPS = 1e-5
LOG2E = math.log2(math.e)
DIFF_QSCALE = (DIFF_HEAD_DIM ** -0.5) * LOG2E

V7X_LANES = 128
V7X_VMEM_BYTES = 64 * 1024 * 1024
VMEM_LIMIT_BYTES = 56 * 1024 * 1024

COL_GQ, COL_GK, COL_GV, COL_GG = 0, 512, 1024, 2048
COL_DQ, COL_DK, COL_DV = 3072, 4096, 5120
MAIN_COLS = 6144

GLA_CHUNK = 256
GLA_LEVELS = int(math.log2(GLA_CHUNK))
GLA_FAST_MAX_DECAY = 64.0
Q_TILE = 256
ROW_CHUNK = 256
SCORE_BOUND_NO_SHIFT = 60.0


def _cparams(sem):
    return pltpu.CompilerParams(dimension_semantics=sem, vmem_limit_bytes=VMEM_LIMIT_BYTES)


def _log_decay(z):
    return (jnp.minimum(z, 0.0) - jnp.log1p(jnp.exp(-jnp.abs(z)))) * (1.0 / GLA_GATE_TAU)


def _log_decay_split(z):
    la = _log_decay(z)
    la_hi = la.astype(BF16)
    la_lo = (la - la_hi.astype(F32)).astype(BF16)
    return jnp.concatenate([la_hi, la_lo], axis=1)


def _cumsum_rows(x):
    n = x.shape[0]
    row = lax.broadcasted_iota(jnp.int32, (n, 1), 0)
    s = 1
    while s < n:
        x = x + jnp.where(row >= s, pltpu.roll(x, s, 0), 0.0)
        s *= 2
    return x


def _in_proj_kernel(x_ref, nw_ref, wa_ref, wb_ref, wlr_ref, cos_ref, sin_ref,
                    qnw_ref, qnr_ref, knw_ref, knr_ref, w2_ref, gb_ref,
                    o_ref, olr_ref, b2_ref):
    tm = x_ref.shape[0]
    na = wa_ref.shape[1]
    dh = DIFF_HEAD_DIM
    n_half_heads = DIFF_WIDTH // dh
    for r0 in range(0, tm, ROW_CHUNK):
        rows = slice(r0, r0 + ROW_CHUNK)
        x = x_ref[rows, :]
        ms = jnp.mean(x * x, axis=-1, keepdims=True)
        n = (x * lax.rsqrt(ms + NORM_EPS) * nw_ref[...]).astype(BF16)
        o_ref[rows, 0:na] = jnp.dot(n, wa_ref[...], preferred_element_type=F32).astype(BF16)
        lr = jnp.dot(n, wlr_ref[...], preferred_element_type=F32).astype(BF16)
        olr_ref[rows, :] = lr
        z = jnp.dot(lr, w2_ref[...], preferred_element_type=F32) + gb_ref[...]
        for h in range(GLA_HEADS):
            cols = slice(h * GLA_HEAD_K, (h + 1) * GLA_HEAD_K)
            b2_ref[rows, cols] = _cumsum_rows(_log_decay(z[:, cols]) * LOG2E)
        ob = jnp.dot(n, wb_ref[...], preferred_element_type=F32)

        cosf = jnp.concatenate([cos_ref[rows, :], cos_ref[rows, :]], axis=1)
        sinf = jnp.concatenate([-sin_ref[rows, :], sin_ref[rows, :]], axis=1)
        tabs = ((cosf * (qnw_ref[...] * DIFF_QSCALE), sinf * (qnr_ref[...] * DIFF_QSCALE)),
                (cosf * knw_ref[...], sinf * knr_ref[...]))
        for i in range(2 * n_half_heads):
            ta, tb = tabs[i // n_half_heads]
            t = ob[:, i * dh:(i + 1) * dh]
            r = lax.rsqrt(jnp.mean(t * t, axis=-1, keepdims=True) + NORM_EPS)
            o_ref[rows, na + i * dh:na + (i + 1) * dh] = (
                r * (t * ta + pltpu.roll(t, dh // 2, 1) * tb)).astype(BF16)
        o_ref[rows, na + 2 * DIFF_WIDTH:] = ob[:, 2 * DIFF_WIDTH:].astype(BF16)


def _in_proj(x2, norm_w, w_a, w_b, w_lr, cos, sin, qnw, qnr, knw, knr, w2p, gate_b, layer, tm):
    assert ROW_CHUNK == GLA_CHUNK
    m, d = x2.shape
    resident = pl.Buffered(1)
    half = DIFF_HEAD_DIM // 2
    vec = pl.BlockSpec((1, DIFF_HEAD_DIM), lambda i: (0, 0))
    return pl.pallas_call(
        _in_proj_kernel,
        grid=(m // tm,),
        in_specs=[
            pl.BlockSpec((tm, d), lambda i: (i, 0)),
            pl.BlockSpec((1, d), lambda i: (0, 0)),
            pl.BlockSpec((None, d, w_a.shape[2]), lambda i: (layer, 0, 0), pipeline_mode=resident),
            pl.BlockSpec((None, d, w_b.shape[2]), lambda i: (layer, 0, 0), pipeline_mode=resident),
            pl.BlockSpec((None, d, V7X_LANES), lambda i: (layer, 0, 0), pipeline_mode=resident),
            pl.BlockSpec((tm, half), lambda i: (i, 0)),
            pl.BlockSpec((tm, half), lambda i: (i, 0)),
            vec, vec, vec, vec,
            pl.BlockSpec((V7X_LANES, GLA_KEY_WIDTH), lambda i: (0, 0)),
            pl.BlockSpec((1, GLA_KEY_WIDTH), lambda i: (0, 0)),
        ],
        out_specs=[
            pl.BlockSpec((tm, MAIN_COLS), lambda i: (i, 0)),
            pl.BlockSpec((tm, V7X_LANES), lambda i: (i, 0)),
            pl.BlockSpec((tm, GLA_KEY_WIDTH), lambda i: (i, 0)),
        ],
        out_shape=[
            jax.ShapeDtypeStruct((m, MAIN_COLS), BF16),
            jax.ShapeDtypeStruct((m, V7X_LANES), BF16),
            jax.ShapeDtypeStruct((m, GLA_KEY_WIDTH), F32),
        ],
        compiler_params=_cparams(("parallel",)),
        name="in_proj",
    )(x2, norm_w, w_a, w_b, w_lr, cos, sin, qnw, qnr, knw, knr, w2p, gate_b)


def _gla_constants():
    c = GLA_CHUNK
    i = np.arange(c)[:, None]
    t = np.arange(c)[None, :]
    mats = [(t <= i), (t > i)]
    for lvl in range(GLA_LEVELS):
        s = 1 << lvl
        m = (i // (2 * s)) * (2 * s) + s - 1
        mats.append((t > np.minimum(i, m)) & (t <= np.maximum(i, m)))
    wstack = np.concatenate(mats, axis=0).astype(np.float32)
    x = i ^ t
    level = np.where(i > t, np.floor(np.log2(np.maximum(x, 1))), np.where(i == t, -1, -2))
    return wstack, level.astype(np.int32)


def _gla_kernel(q_ref, k_ref, v_ref, b2_ref, lr_ref, w2_ref, gb_ref, wstk_ref, lvl_ref,
                o_ref, state_ref, *, seq):
    c = GLA_CHUNK
    dk = GLA_HEAD_K
    n_chunks = seq // c
    w2 = w2_ref[...]
    gb = gb_ref[...]
    eye = (lax.broadcasted_iota(jnp.int32, (dk, dk), 0)
           == lax.broadcasted_iota(jnp.int32, (dk, dk), 1))
    nt = (((1,), (1,)), ((), ()))
    tn = (((0,), (0,)), ((), ()))

    def log_decay(r0):
        z = jnp.dot(lr_ref[pl.ds(r0, c), :], w2, preferred_element_type=F32) + gb
        return _log_decay_split(z)

    def finish(r0, o, kd, vb, e_last, state):
        o_ref[pl.ds(r0, c), :] = o.astype(o_ref.dtype)
        e_col = jnp.sum(jnp.where(eye, e_last, 0.0), axis=1, keepdims=True)
        state_ref[...] = state * e_col + lax.dot_general(kd, vb, tn, preferred_element_type=F32)

    tot = jnp.zeros((1, dk), F32)
    for ci in range(n_chunks):
        tot = jnp.maximum(tot, -b2_ref[(ci + 1) * c - 1:(ci + 1) * c, :])
    bounded = jnp.max(tot, axis=-1, keepdims=True)[0, 0] <= GLA_FAST_MAX_DECAY
    state_ref[...] = jnp.zeros_like(state_ref)

    def fast_chunk(ci, carry):
        r0 = pl.multiple_of(ci * c, c)
        qf = q_ref[pl.ds(r0, c), :].astype(F32)
        kf = k_ref[pl.ds(r0, c), :].astype(F32)
        vb = v_ref[pl.ds(r0, c), :]
        b2 = b2_ref[pl.ds(r0, c), :]
        eq = jnp.exp2(b2)
        qt = (qf * eq).astype(BF16)
        kt = kf * jnp.exp2(-b2)
        state = state_ref[...]
        causal = (lax.broadcasted_iota(jnp.int32, (c, c), 0)
                  >= lax.broadcasted_iota(jnp.int32, (c, c), 1))
        p = lax.dot_general(qt, kt.astype(BF16), nt, preferred_element_type=F32)
        o = jnp.dot(qt, state.astype(BF16), preferred_element_type=F32)
        o = o + jnp.dot(jnp.where(causal, p, 0.0).astype(BF16), vb, preferred_element_type=F32)
        e_last = eq[c - 1:c, :]
        finish(r0, o, (kt * e_last).astype(BF16), vb, e_last, state)
        return carry

    def level_chunk(ci, carry):
        r0 = pl.multiple_of(ci * c, c)
        lvl = lvl_ref[...]
        row = lax.broadcasted_iota(jnp.int32, (c, 1), 0)
        qf = q_ref[pl.ds(r0, c), :].astype(F32)
        kf = k_ref[pl.ds(r0, c), :].astype(F32)
        kb = k_ref[pl.ds(r0, c), :]
        vb = v_ref[pl.ds(r0, c), :]
        sums = jnp.dot(wstk_ref[...], log_decay(r0), preferred_element_type=F32)
        sums = sums[:, :dk] + sums[:, dk:]
        b = sums[0:c]
        b_rev = sums[c:2 * c]
        state = state_ref[...]
        o = jnp.dot((qf * jnp.exp(b)).astype(BF16), state.astype(BF16),
                    preferred_element_type=F32)
        attn = jnp.where(lvl == -1,
                         lax.dot_general(qf.astype(BF16), kb, nt, preferred_element_type=F32), 0.0)
        for l in range(GLA_LEVELS):
            e = jnp.exp(sums[(l + 2) * c:(l + 3) * c])
            upper = ((row >> l) & 1) == 1
            qt = jnp.where(upper, qf * e, 0.0).astype(BF16)
            kt = jnp.where(upper, 0.0, kf * e).astype(BF16)
            p = lax.dot_general(qt, kt, nt, preferred_element_type=F32)
            attn = jnp.where(lvl == l, p, attn)
        o = o + jnp.dot(attn.astype(BF16), vb, preferred_element_type=F32)
        finish(r0, o, (kf * jnp.exp(b_rev)).astype(BF16), vb, jnp.exp(b[c - 1:c, :]), state)
        return carry

    @pl.when(bounded)
    def _():
        lax.fori_loop(0, n_chunks, fast_chunk, 0, unroll=4)

    @pl.when(jnp.logical_not(bounded))
    def _():
        lax.fori_loop(0, n_chunks, level_chunk, 0)


def _gla(proj, b2, lr, w2p, gate_b, wstack, level, batch, seq):
    m = proj.shape[0]
    kq = GLA_HEAD_K
    kv = GLA_HEAD_V
    rows = wstack.shape[0]
    return pl.pallas_call(
        functools.partial(_gla_kernel, seq=seq),
        grid=(batch, GLA_HEADS),
        in_specs=[
            pl.BlockSpec((seq, kq), lambda b, h: (b, COL_GQ // kq + h)),
            pl.BlockSpec((seq, kq), lambda b, h: (b, COL_GK // kq + h)),
            pl.BlockSpec((seq, kv), lambda b, h: (b, COL_GV // kv + h)),
            pl.BlockSpec((seq, kq), lambda b, h: (b, h)),
            pl.BlockSpec((seq, V7X_LANES), lambda b, h: (b, 0)),
            pl.BlockSpec((V7X_LANES, kq), lambda b, h: (0, h)),
            pl.BlockSpec((1, kq), lambda b, h: (0, h)),
            pl.BlockSpec((rows, GLA_CHUNK), lambda b, h: (0, 0)),
            pl.BlockSpec((GLA_CHUNK, GLA_CHUNK), lambda b, h: (0, 0)),
        ],
        out_specs=pl.BlockSpec((seq, kv), lambda b, h: (b, h)),
        out_shape=jax.ShapeDtypeStruct((m, GLA_WIDTH), BF16),
        scratch_shapes=[pltpu.VMEM((kq, kv), F32)],
        compiler_params=_cparams(("parallel", "parallel")),
        name="gla",
    )(proj, proj, proj, b2, lr, w2p, gate_b, wstack, level)


def _diff_kernel(q_ref, k_ref, v_ref, qnw_ref, knw_ref, lq1_ref, lk1_ref, lq2_ref, lk2_ref,
                 o_ref, *, seq, lam_init):
    dh = DIFF_HEAD_DIM
    tq = Q_TILE
    lam = (jnp.exp(jnp.sum(lq1_ref[...] * lk1_ref[...], axis=-1, keepdims=True))
           - jnp.exp(jnp.sum(lq2_ref[...] * lk2_ref[...], axis=-1, keepdims=True))
           + lam_init)

    nt = (((1,), (1,)), ((), ()))
    tri = (lax.broadcasted_iota(jnp.int32, (tq, tq), 0)
           >= lax.broadcasted_iota(jnp.int32, (tq, tq), 1))

    def attend(subtract_max):
        for t in reversed(range(seq // tq)):
            r0 = t * tq
            ps, ls = [], []
            for comp in range(2):
                cols = slice(comp * dh, (comp + 1) * dh)
                q = q_ref[r0:r0 + tq, cols]
                s = lax.dot_general(q, k_ref[r0:r0 + tq, cols], nt, preferred_element_type=F32)
                s = jnp.where(tri, s, -jnp.inf)
                if t > 0:
                    s_off = lax.dot_general(q, k_ref[0:r0, cols], nt,
                                            preferred_element_type=F32)
                    s = jnp.concatenate([s_off, s], axis=1)
                if subtract_max:
                    s = s - jnp.max(s, axis=-1, keepdims=True)
                p = jnp.exp2(s)
                ls.append(jnp.sum(p, axis=-1, keepdims=True))
                ps.append(p.astype(BF16))
            ratio = (lam * ls[0] / ls[1]).astype(BF16)
            w = ps[0] - ratio * ps[1]
            o = jnp.dot(w, v_ref[0:r0 + tq, :], preferred_element_type=F32) * (1.0 / ls[0])
            o_ref[r0:r0 + tq, :] = o.astype(o_ref.dtype)

    bound = (dh * DIFF_QSCALE) * (jnp.max(jnp.abs(qnw_ref[...]), axis=-1, keepdims=True)
                             * jnp.max(jnp.abs(knw_ref[...]), axis=-1, keepdims=True))
    small = bound[0, 0] <= SCORE_BOUND_NO_SHIFT

    @pl.when(small)
    def _():
        attend(False)

    @pl.when(jnp.logical_not(small))
    def _():
        attend(True)


def _diff_attn(proj, qnw, knw, lq1, lk1, lq2, lk2, batch, seq, lam_init):
    m = proj.shape[0]
    hw = 2 * DIFF_HEAD_DIM
    vec = pl.BlockSpec((1, DIFF_HEAD_DIM), lambda b, h: (0, 0))
    return pl.pallas_call(
        functools.partial(_diff_kernel, seq=seq, lam_init=lam_init),
        grid=(batch, DIFF_HEADS),
        in_specs=[
            pl.BlockSpec((seq, hw), lambda b, h: (b, COL_DQ // hw + h)),
            pl.BlockSpec((seq, hw), lambda b, h: (b, COL_DK // hw + h)),
            pl.BlockSpec((seq, hw), lambda b, h: (b, COL_DV // hw + h)),
            vec, vec, vec, vec, vec, vec,
        ],
        out_specs=pl.BlockSpec((seq, hw), lambda b, h: (b, h)),
        out_shape=jax.ShapeDtypeStruct((m, DIFF_WIDTH), BF16),
        compiler_params=_cparams(("parallel", "parallel")),
        name="diff_attn",
    )(proj, proj, proj, qnw, knw, lq1, lk1, lq2, lk2)


def _out_proj_kernel(x_ref, og_ref, g_ref, od_ref, onw_ref, sw_ref, wg_ref, wd_ref, o_ref,
                     *, lam_init):
    tm = x_ref.shape[0]
    hv = GLA_HEAD_V
    hd = 2 * DIFF_HEAD_DIM
    gla_eps = NORM_EPS * GLA_HEAD_K
    onw = onw_ref[...]
    sw = sw_ref[...] * (1.0 - lam_init)
    for r0 in range(0, tm, ROW_CHUNK):
        rows = slice(r0, r0 + ROW_CHUNK)
        ys = []
        for h in range(GLA_HEADS):
            o = og_ref[rows, h * hv:(h + 1) * hv].astype(F32)
            g = g_ref[rows, h * hv:(h + 1) * hv].astype(F32)
            y = o * lax.rsqrt(jnp.mean(o * o, axis=-1, keepdims=True) + gla_eps) * onw
            ys.append((y * (g * jax.nn.sigmoid(g))).astype(BF16))
        acc = jnp.dot(jnp.concatenate(ys, axis=1), wg_ref[...], preferred_element_type=F32)
        ys = []
        for h in range(DIFF_HEADS):
            o = od_ref[rows, h * hd:(h + 1) * hd].astype(F32)
            y = o * lax.rsqrt(jnp.mean(o * o, axis=-1, keepdims=True) + SUBLN_EPS) * sw
            ys.append(y.astype(BF16))
        acc = acc + jnp.dot(jnp.concatenate(ys, axis=1), wd_ref[...],
                            preferred_element_type=F32)
        o_ref[rows, :] = x_ref[rows, :] + acc


def _out_proj(x2, og, proj, od, out_norm_w, subln_w, w_out, layer, tm, lam_init):
    m, d = x2.shape
    return pl.pallas_call(
        functools.partial(_out_proj_kernel, lam_init=lam_init),
        grid=(m // tm,),
        in_specs=[
            pl.BlockSpec((tm, d), lambda i: (i, 0)),
            pl.BlockSpec((tm, GLA_WIDTH), lambda i: (i, 0)),
            pl.BlockSpec((tm, GLA_WIDTH), lambda i: (i, COL_GG // GLA_WIDTH)),
            pl.BlockSpec((tm, DIFF_WIDTH), lambda i: (i, 0)),
            pl.BlockSpec((1, GLA_HEAD_V), lambda i: (0, 0)),
            pl.BlockSpec((1, 2 * DIFF_HEAD_DIM), lambda i: (0, 0)),
            pl.BlockSpec((None, GLA_WIDTH, d), lambda i: (layer, 0, 0)),
            pl.BlockSpec((None, DIFF_WIDTH, d), lambda i: (layer, GLA_WIDTH // DIFF_WIDTH, 0)),
        ],
        out_specs=pl.BlockSpec((tm, d), lambda i: (i, 0)),
        out_shape=jax.ShapeDtypeStruct((m, d), F32),
        compiler_params=_cparams(("parallel",)),
        name="out_proj",
    )(x2, og, proj, od, out_norm_w, subln_w, w_out, w_out)


def _mlp_kernel(x_ref, nw_ref, wu_ref, wd_ref, o_ref, n_ref):
    j = pl.program_id(1)
    tm = x_ref.shape[0]

    def ff(n):
        h = jnp.dot(n, wu_ref[...], preferred_element_type=F32)
        h = jnp.square(jnp.maximum(h, 0.0)).astype(BF16)
        return jnp.dot(h, wd_ref[...], preferred_element_type=F32)

    @pl.when(j == 0)
    def _():
        for r0 in range(0, tm, ROW_CHUNK):
            x = x_ref[r0:r0 + ROW_CHUNK, :]
            ms = jnp.mean(x * x, axis=-1, keepdims=True)
            n = (x * lax.rsqrt(ms + NORM_EPS) * nw_ref[...]).astype(BF16)
            n_ref[r0:r0 + ROW_CHUNK, :] = n
            o_ref[r0:r0 + ROW_CHUNK, :] = x + ff(n)

    @pl.when(j > 0)
    def _():
        o_ref[...] += ff(n_ref[...])


def _mlp(x2, norm_w, w_up, w_down, layer, tm, tf):
    m, d = x2.shape
    f = w_up.shape[2]
    return pl.pallas_call(
        _mlp_kernel,
        grid=(m // tm, f // tf),
        in_specs=[
            pl.BlockSpec((tm, d), lambda i, j: (i, 0)),
            pl.BlockSpec((1, d), lambda i, j: (0, 0)),
            pl.BlockSpec((None, d, tf), lambda i, j: (layer, 0, j)),
            pl.BlockSpec((None, tf, d), lambda i, j: (layer, j, 0)),
        ],
        out_specs=pl.BlockSpec((tm, d), lambda i, j: (i, 0)),
        out_shape=jax.ShapeDtypeStruct((m, d), F32),
        scratch_shapes=[pltpu.VMEM((tm, d), BF16)],
        compiler_params=_cparams(("parallel", "arbitrary")),
        name="mlp",
    )(x2, norm_w, w_up, w_down)


def _lambda_init(layer_idx):
    return 0.8 - 0.6 * math.exp(-0.3 * layer_idx)


def kernel(x, positions, attn_norm_w, w_in, gla_gate_w2, gla_gate_b, gla_out_norm_w,
           diff_q_norm_w, diff_k_norm_w, diff_lambda_q1, diff_lambda_k1, diff_lambda_q2,
           diff_lambda_k2, diff_subln_w, w_out, mlp_norm_w, w_up, w_down):
    batch, seq, d = x.shape
    depth = w_in.shape[0]
    m = batch * seq
    assert d == GLA_WIDTH + DIFF_WIDTH and seq % GLA_CHUNK == 0 and seq % Q_TILE == 0
    tm_proj = min(512, m)
    tm_out = min(512, m)
    tm_mlp = min(512, m)

    half = DIFF_HEAD_DIM // 2
    inv_freq = ROPE_THETA ** (-jnp.arange(0, DIFF_HEAD_DIM, 2, dtype=F32) / DIFF_HEAD_DIM)
    ang = (positions.astype(F32)[..., None] * inv_freq).reshape(m, half)
    cos, sin = jnp.cos(ang), jnp.sin(ang)

    wstack_np, level_np = _gla_constants()
    wstack = jnp.asarray(wstack_np, dtype=BF16)
    level = jnp.asarray(level_np)

    sizes = (GLA_KEY_WIDTH, GLA_KEY_WIDTH, GLA_WIDTH, GLA_WIDTH, GLA_GATE_RANK,
             DIFF_WIDTH, DIFF_WIDTH, DIFF_WIDTH)
    offs = np.concatenate([[0], np.cumsum(sizes)])
    lr0, lr1 = int(offs[4]), int(offs[5])

    w_a = w_in[:, :, :lr0].astype(BF16)
    w_b = w_in[:, :, lr1:].astype(BF16)
    w_lr = jnp.pad(w_in[:, :, lr0:lr1],
                   ((0, 0), (0, 0), (0, V7X_LANES - GLA_GATE_RANK))).astype(BF16)
    w_out_b = w_out.astype(BF16)
    w_up_b = w_up.astype(BF16)
    w_down_b = w_down.astype(BF16)

    x2 = x.reshape(m, d)
    for layer in range(depth):
        w2p = jnp.pad(gla_gate_w2[layer], ((0, V7X_LANES - GLA_GATE_RANK), (0, 0))).astype(BF16)
        gate_b = gla_gate_b[layer][None, :]

        qnw = diff_q_norm_w[layer][None, :]
        knw = diff_k_norm_w[layer][None, :]
        proj, lr, b2 = _in_proj(x2, attn_norm_w[layer][None, :], w_a, w_b, w_lr, cos, sin,
                                qnw, jnp.roll(qnw, half, axis=1), knw, jnp.roll(knw, half, axis=1),
                                w2p, gate_b, layer, tm_proj)
        o_gla = _gla(proj, b2, lr, w2p, gate_b, wstack, level, batch, seq)
        o_diff = _diff_attn(proj, qnw, knw,
                            diff_lambda_q1[layer][None, :], diff_lambda_k1[layer][None, :],
                            diff_lambda_q2[layer][None, :], diff_lambda_k2[layer][None, :],
                            batch, seq, _lambda_init(layer))
        x2 = _out_proj(x2, o_gla, proj, o_diff, gla_out_norm_w[layer][None, :],
                       diff_subln_w[layer][None, :], w_out_b, layer, tm_out, _lambda_init(layer))
        x2 = _mlp(x2, mlp_norm_w[layer][None, :], w_up_b, w_down_b, layer, tm_mlp, 1024)
    return x2.reshape(batch, seq, d)
```

```python
import functools
import math

import numpy as np
import jax
import jax.numpy as jnp
from jax import lax
from jax.experimental import pallas as pl
from jax.experimental.pallas import tpu as pltpu

F32 = jnp.float32
BF16 = jnp.bfloat16

GLA_HEADS = 4
GLA_HEAD_K = 128
GLA_HEAD_V = 256
GLA_KEY_WIDTH = GLA_HEADS * GLA_HEAD_K
GLA_WIDTH = GLA_HEADS * GLA_HEAD_V
GLA_GATE_RANK = 16
GLA_GATE_TAU = 16.0
DIFF_HEADS = 4
DIFF_HEAD_DIM = 128
DIFF_WIDTH = DIFF_HEADS * 2 * DIFF_HEAD_DIM
ROPE_THETA = 10000.0
NORM_EPS = 1e-6
SUBLN_EPS = 1e-5
LOG2E = math.log2(math.e)
DIFF_QSCALE = (DIFF_HEAD_DIM ** -0.5) * LOG2E

V7X_LANES = 128
V7X_VMEM_BYTES = 64 * 1024 * 1024
VMEM_LIMIT_BYTES = 56 * 1024 * 1024

COL_GQ, COL_GK, COL_GV, COL_GG = 0, 512, 1024, 2048
COL_DQ, COL_DK, COL_DV = 3072, 4096, 5120
MAIN_COLS = 6144

GLA_CHUNK = 256
GLA_LEVELS = int(math.log2(GLA_CHUNK))
GLA_FAST_MAX_DECAY = 64.0
Q_TILE = 256
ROW_CHUNK = 256
SCORE_BOUND_NO_SHIFT = 60.0


def _cparams(sem):
    return pltpu.CompilerParams(dimension_semantics=sem, vmem_limit_bytes=VMEM_LIMIT_BYTES)


def _log_decay(z):
    return (jnp.minimum(z, 0.0) - jnp.log1p(jnp.exp(-jnp.abs(z)))) * (1.0 / GLA_GATE_TAU)


def _log_decay_split(z):
    la = _log_decay(z)
    la_hi = la.astype(BF16)
    la_lo = (la - la_hi.astype(F32)).astype(BF16)
    return jnp.concatenate([la_hi, la_lo], axis=1)


def _cumsum_rows(x):
    n = x.shape[0]
    row = lax.broadcasted_iota(jnp.int32, (n, 1), 0)
    s = 1
    while s < n:
        x = x + jnp.where(row >= s, pltpu.roll(x, s, 0), 0.0)
        s *= 2
    return x


def _in_proj_kernel(x_ref, nw_ref, wa_ref, wb_ref, wlr_ref, cos_ref, sin_ref,
                    qnw_ref, qnr_ref, knw_ref, knr_ref, w2_ref, gb_ref,
                    o_ref, olr_ref, b2_ref):
    tm = x_ref.shape[0]
    na = wa_ref.shape[1]
    dh = DIFF_HEAD_DIM
    n_half_heads = DIFF_WIDTH // dh
    for r0 in range(0, tm, ROW_CHUNK):
        rows = slice(r0, r0 + ROW_CHUNK)
        x = x_ref[rows, :]
        ms = jnp.mean(x * x, axis=-1, keepdims=True)
        n = (x * lax.rsqrt(ms + NORM_EPS) * nw_ref[...]).astype(BF16)
        o_ref[rows, 0:na] = jnp.dot(n, wa_ref[...], preferred_element_type=F32).astype(BF16)
        lr = jnp.dot(n, wlr_ref[...], preferred_element_type=F32).astype(BF16)
        olr_ref[rows, :] = lr
        z = jnp.dot(lr, w2_ref[...], preferred_element_type=F32) + gb_ref[...]
        for h in range(GLA_HEADS):
            cols = slice(h * GLA_HEAD_K, (h + 1) * GLA_HEAD_K)
            b2_ref[rows, cols] = _cumsum_rows(_log_decay(z[:, cols]) * LOG2E)
        ob = jnp.dot(n, wb_ref[...], preferred_element_type=F32)

        cosf = jnp.concatenate([cos_ref[rows, :], cos_ref[rows, :]], axis=1)
        sinf = jnp.concatenate([-sin_ref[rows, :], sin_ref[rows, :]], axis=1)
        tabs = ((cosf * (qnw_ref[...] * DIFF_QSCALE), sinf * (qnr_ref[...] * DIFF_QSCALE)),
                (cosf * knw_ref[...], sinf * knr_ref[...]))
        for i in range(2 * n_half_heads):
            ta, tb = tabs[i // n_half_heads]
            t = ob[:, i * dh:(i + 1) * dh]
            r = lax.rsqrt(jnp.mean(t * t, axis=-1, keepdims=True) + NORM_EPS)
            o_ref[rows, na + i * dh:na + (i + 1) * dh] = (
                r * (t * ta + pltpu.roll(t, dh // 2, 1) * tb)).astype(BF16)
        o_ref[rows, na + 2 * DIFF_WIDTH:] = ob[:, 2 * DIFF_WIDTH:].astype(BF16)


def _in_proj(x2, norm_w, w_a, w_b, w_lr, cos, sin, qnw, qnr, knw, knr, w2p, gate_b, layer, tm):
    assert ROW_CHUNK == GLA_CHUNK
    m, d = x2.shape
    resident = pl.Buffered(1)
    half = DIFF_HEAD_DIM // 2
    vec = pl.BlockSpec((1, DIFF_HEAD_DIM), lambda i: (0, 0))
    return pl.pallas_call(
        _in_proj_kernel,
        grid=(m // tm,),
        in_specs=[
            pl.BlockSpec((tm, d), lambda i: (i, 0)),
            pl.BlockSpec((1, d), lambda i: (0, 0)),
            pl.BlockSpec((None, d, w_a.shape[2]), lambda i: (layer, 0, 0), pipeline_mode=resident),
            pl.BlockSpec((None, d, w_b.shape[2]), lambda i: (layer, 0, 0), pipeline_mode=resident),
            pl.BlockSpec((None, d, V7X_LANES), lambda i: (layer, 0, 0), pipeline_mode=resident),
            pl.BlockSpec((tm, half), lambda i: (i, 0)),
            pl.BlockSpec((tm, half), lambda i: (i, 0)),
            vec, vec, vec, vec,
            pl.BlockSpec((V7X_LANES, GLA_KEY_WIDTH), lambda i: (0, 0)),
            pl.BlockSpec((1, GLA_KEY_WIDTH), lambda i: (0, 0)),
        ],
        out_specs=[
            pl.BlockSpec((tm, MAIN_COLS), lambda i: (i, 0)),
            pl.BlockSpec((tm, V7X_LANES), lambda i: (i, 0)),
            pl.BlockSpec((tm, GLA_KEY_WIDTH), lambda i: (i, 0)),
        ],
        out_shape=[
            jax.ShapeDtypeStruct((m, MAIN_COLS), BF16),
            jax.ShapeDtypeStruct((m, V7X_LANES), BF16),
            jax.ShapeDtypeStruct((m, GLA_KEY_WIDTH), F32),
        ],
        compiler_params=_cparams(("parallel",)),
        name="in_proj",
    )(x2, norm_w, w_a, w_b, w_lr, cos, sin, qnw, qnr, knw, knr, w2p, gate_b)


def _gla_constants():
    c = GLA_CHUNK
    i = np.arange(c)[:, None]
    t = np.arange(c)[None, :]
    mats = [(t <= i), (t > i)]
    for lvl in range(GLA_LEVELS):
        s = 1 << lvl
        m = (i // (2 * s)) * (2 * s) + s - 1
        mats.append((t > np.minimum(i, m)) & (t <= np.maximum(i, m)))
    wstack = np.concatenate(mats, axis=0).astype(np.float32)
    x = i ^ t
    level = np.where(i > t, np.floor(np.log2(np.maximum(x, 1))), np.where(i == t, -1, -2))
    return wstack, level.astype(np.int32)


def _gla_kernel(q_ref, k_ref, v_ref, b2_ref, lr_ref, w2_ref, gb_ref, wstk_ref, lvl_ref,
                o_ref, state_ref, *, seq):
    c = GLA_CHUNK
    dk = GLA_HEAD_K
    n_chunks = seq // c
    w2 = w2_ref[...]
    gb = gb_ref[...]
    eye = (lax.broadcasted_iota(jnp.int32, (dk, dk), 0)
           == lax.broadcasted_iota(jnp.int32, (dk, dk), 1))
    nt = (((1,), (1,)), ((), ()))
    tn = (((0,), (0,)), ((), ()))

    def log_decay(r0):
        z = jnp.dot(lr_ref[pl.ds(r0, c), :], w2, preferred_element_type=F32) + gb
        return _log_decay_split(z)

    def finish(r0, o, kd, vb, e_last, state):
        o_ref[pl.ds(r0, c), :] = o.astype(o_ref.dtype)
        e_col = jnp.sum(jnp.where(eye, e_last, 0.0), axis=1, keepdims=True)
        state_ref[...] = state * e_col + lax.dot_general(kd, vb, tn, preferred_element_type=F32)

    tot = jnp.zeros((1, dk), F32)
    for ci in range(n_chunks):
        tot = jnp.maximum(tot, -b2_ref[(ci + 1) * c - 1:(ci + 1) * c, :])
    bounded = jnp.max(tot, axis=-1, keepdims=True)[0, 0] <= GLA_FAST_MAX_DECAY
    state_ref[...] = jnp.zeros_like(state_ref)

    def fast_chunk(ci, carry):
        r0 = pl.multiple_of(ci * c, c)
        qf = q_ref[pl.ds(r0, c), :].astype(F32)
        kf = k_ref[pl.ds(r0, c), :].astype(F32)
        vb = v_ref[pl.ds(r0, c), :]
        b2 = b2_ref[pl.ds(r0, c), :]
        eq = jnp.exp2(b2)
        qt = (qf * eq).astype(BF16)
        kt = kf * jnp.exp2(-b2)
        state = state_ref[...]
        causal = (lax.broadcasted_iota(jnp.int32, (c, c), 0)
                  >= lax.broadcasted_iota(jnp.int32, (c, c), 1))
        p = lax.dot_general(qt, kt.astype(BF16), nt, preferred_element_type=F32)
        o = jnp.dot(qt, state.astype(BF16), preferred_element_type=F32)
        o = o + jnp.dot(jnp.where(causal, p, 0.0).astype(BF16), vb, preferred_element_type=F32)
        e_last = eq[c - 1:c, :]
        finish(r0, o, (kt * e_last).astype(BF16), vb, e_last, state)
        return carry

    def level_chunk(ci, carry):
        r0 = pl.multiple_of(ci * c, c)
        lvl = lvl_ref[...]
        row = lax.broadcasted_iota(jnp.int32, (c, 1), 0)
        qf = q_ref[pl.ds(r0, c), :].astype(F32)
        kf = k_ref[pl.ds(r0, c), :].astype(F32)
        kb = k_ref[pl.ds(r0, c), :]
        vb = v_ref[pl.ds(r0, c), :]
        sums = jnp.dot(wstk_ref[...], log_decay(r0), preferred_element_type=F32)
        sums = sums[:, :dk] + sums[:, dk:]
        b = sums[0:c]
        b_rev = sums[c:2 * c]
        state = state_ref[...]
        o = jnp.dot((qf * jnp.exp(b)).astype(BF16), state.astype(BF16),
                    preferred_element_type=F32)
        attn = jnp.where(lvl == -1,
                         lax.dot_general(qf.astype(BF16), kb, nt, preferred_element_type=F32), 0.0)
        for l in range(GLA_LEVELS):
            e = jnp.exp(sums[(l + 2) * c:(l + 3) * c])
            upper = ((row >> l) & 1) == 1
            qt = jnp.where(upper, qf * e, 0.0).astype(BF16)
            kt = jnp.where(upper, 0.0, kf * e).astype(BF16)
            p = lax.dot_general(qt, kt, nt, preferred_element_type=F32)
            attn = jnp.where(lvl == l, p, attn)
        o = o + jnp.dot(attn.astype(BF16), vb, preferred_element_type=F32)
        finish(r0, o, (kf * jnp.exp(b_rev)).astype(BF16), vb, jnp.exp(b[c - 1:c, :]), state)
        return carry

    @pl.when(bounded)
    def _():
        lax.fori_loop(0, n_chunks, fast_chunk, 0, unroll=4)

    @pl.when(jnp.logical_not(bounded))
    def _():
        lax.fori_loop(0, n_chunks, level_chunk, 0)


def _gla(proj, b2, lr, w2p, gate_b, wstack, level, batch, seq):
    m = proj.shape[0]
    kq = GLA_HEAD_K
    kv = GLA_HEAD_V
    rows = wstack.shape[0]
    return pl.pallas_call(
        functools.partial(_gla_kernel, seq=seq),
        grid=(batch, GLA_HEADS),
        in_specs=[
            pl.BlockSpec((seq, kq), lambda b, h: (b, COL_GQ // kq + h)),
            pl.BlockSpec((seq, kq), lambda b, h: (b, COL_GK // kq + h)),
            pl.BlockSpec((seq, kv), lambda b, h: (b, COL_GV // kv + h)),
            pl.BlockSpec((seq, kq), lambda b, h: (b, h)),
            pl.BlockSpec((seq, V7X_LANES), lambda b, h: (b, 0)),
            pl.BlockSpec((V7X_LANES, kq), lambda b, h: (0, h)),
            pl.BlockSpec((1, kq), lambda b, h: (0, h)),
            pl.BlockSpec((rows, GLA_CHUNK), lambda b, h: (0, 0)),
            pl.BlockSpec((GLA_CHUNK, GLA_CHUNK), lambda b, h: (0, 0)),
        ],
        out_specs=pl.BlockSpec((seq, kv), lambda b, h: (b, h)),
        out_shape=jax.ShapeDtypeStruct((m, GLA_WIDTH), BF16),
        scratch_shapes=[pltpu.VMEM((kq, kv), F32)],
        compiler_params=_cparams(("parallel", "parallel")),
        name="gla",
    )(proj, proj, proj, b2, lr, w2p, gate_b, wstack, level)


def _diff_kernel(q_ref, k_ref, v_ref, qnw_ref, knw_ref, lq1_ref, lk1_ref, lq2_ref, lk2_ref,
                 o_ref, *, seq, lam_init):
    dh = DIFF_HEAD_DIM
    tq = Q_TILE
    lam = (jnp.exp(jnp.sum(lq1_ref[...] * lk1_ref[...], axis=-1, keepdims=True))
           - jnp.exp(jnp.sum(lq2_ref[...] * lk2_ref[...], axis=-1, keepdims=True))
           + lam_init)

    nt = (((1,), (1,)), ((), ()))
    tri = (lax.broadcasted_iota(jnp.int32, (tq, tq), 0)
           >= lax.broadcasted_iota(jnp.int32, (tq, tq), 1))

    def attend(subtract_max):
        for t in reversed(range(seq // tq)):
            r0 = t * tq
            ps, ls = [], []
            for comp in range(2):
                cols = slice(comp * dh, (comp + 1) * dh)
                q = q_ref[r0:r0 + tq, cols]
                s = lax.dot_general(q, k_ref[r0:r0 + tq, cols], nt, preferred_element_type=F32)
                s = jnp.where(tri, s, -jnp.inf)
                if t > 0:
                    s_off = lax.dot_general(q, k_ref[0:r0, cols], nt,
                                            preferred_element_type=F32)
                    s = jnp.concatenate([s_off, s], axis=1)
                if subtract_max:
                    s = s - jnp.max(s, axis=-1, keepdims=True)
                p = jnp.exp2(s)
                ls.append(jnp.sum(p, axis=-1, keepdims=True))
                ps.append(p.astype(BF16))
            ratio = (lam * ls[0] / ls[1]).astype(BF16)
            w = ps[0] - ratio * ps[1]
            o = jnp.dot(w, v_ref[0:r0 + tq, :], preferred_element_type=F32) * (1.0 / ls[0])
            o_ref[r0:r0 + tq, :] = o.astype(o_ref.dtype)

    bound = (dh * DIFF_QSCALE) * (jnp.max(jnp.abs(qnw_ref[...]), axis=-1, keepdims=True)
                             * jnp.max(jnp.abs(knw_ref[...]), axis=-1, keepdims=True))
    small = bound[0, 0] <= SCORE_BOUND_NO_SHIFT

    @pl.when(small)
    def _():
        attend(False)

    @pl.when(jnp.logical_not(small))
    def _():
        attend(True)


def _diff_attn(proj, qnw, knw, lq1, lk1, lq2, lk2, batch, seq, lam_init):
    m = proj.shape[0]
    hw = 2 * DIFF_HEAD_DIM
    vec = pl.BlockSpec((1, DIFF_HEAD_DIM), lambda b, h: (0, 0))
    return pl.pallas_call(
        functools.partial(_diff_kernel, seq=seq, lam_init=lam_init),
        grid=(batch, DIFF_HEADS),
        in_specs=[
            pl.BlockSpec((seq, hw), lambda b, h: (b, COL_DQ // hw + h)),
            pl.BlockSpec((seq, hw), lambda b, h: (b, COL_DK // hw + h)),
            pl.BlockSpec((seq, hw), lambda b, h: (b, COL_DV // hw + h)),
            vec, vec, vec, vec, vec, vec,
        ],
        out_specs=pl.BlockSpec((seq, hw), lambda b, h: (b, h)),
        out_shape=jax.ShapeDtypeStruct((m, DIFF_WIDTH), BF16),
        compiler_params=_cparams(("parallel", "parallel")),
        name="diff_attn",
    )(proj, proj, proj, qnw, knw, lq1, lk1, lq2, lk2)


def _out_proj_kernel(x_ref, og_ref, g_ref, od_ref, onw_ref, sw_ref, wg_ref, wd_ref, o_ref,
                     *, lam_init):
    tm = x_ref.shape[0]
    hv = GLA_HEAD_V
    hd = 2 * DIFF_HEAD_DIM
    gla_eps = NORM_EPS * GLA_HEAD_K
    onw = onw_ref[...]
    sw = sw_ref[...] * (1.0 - lam_init)
    for r0 in range(0, tm, ROW_CHUNK):
        rows = slice(r0, r0 + ROW_CHUNK)
        ys = []
        for h in range(GLA_HEADS):
            o = og_ref[rows, h * hv:(h + 1) * hv].astype(F32)
            g = g_ref[rows, h * hv:(h + 1) * hv].astype(F32)
            y = o * lax.rsqrt(jnp.mean(o * o, axis=-1, keepdims=True) + gla_eps) * onw
            ys.append((y * (g * jax.nn.sigmoid(g))).astype(BF16))
        acc = jnp.dot(jnp.concatenate(ys, axis=1), wg_ref[...], preferred_element_type=F32)
        ys = []
        for h in range(DIFF_HEADS):
            o = od_ref[rows, h * hd:(h + 1) * hd].astype(F32)
            y = o * lax.rsqrt(jnp.mean(o * o, axis=-1, keepdims=True) + SUBLN_EPS) * sw
            ys.append(y.astype(BF16))
        acc = acc + jnp.dot(jnp.concatenate(ys, axis=1), wd_ref[...],
                            preferred_element_type=F32)
        o_ref[rows, :] = x_ref[rows, :] + acc


def _out_proj(x2, og, proj, od, out_norm_w, subln_w, w_out, layer, tm, lam_init):
    m, d = x2.shape
    return pl.pallas_call(
        functools.partial(_out_proj_kernel, lam_init=lam_init),
        grid=(m // tm,),
        in_specs=[
            pl.BlockSpec((tm, d), lambda i: (i, 0)),
            pl.BlockSpec((tm, GLA_WIDTH), lambda i: (i, 0)),
            pl.BlockSpec((tm, GLA_WIDTH), lambda i: (i, COL_GG // GLA_WIDTH)),
            pl.BlockSpec((tm, DIFF_WIDTH), lambda i: (i, 0)),
            pl.BlockSpec((1, GLA_HEAD_V), lambda i: (0, 0)),
            pl.BlockSpec((1, 2 * DIFF_HEAD_DIM), lambda i: (0, 0)),
            pl.BlockSpec((None, GLA_WIDTH, d), lambda i: (layer, 0, 0)),
            pl.BlockSpec((None, DIFF_WIDTH, d), lambda i: (layer, GLA_WIDTH // DIFF_WIDTH, 0)),
        ],
        out_specs=pl.BlockSpec((tm, d), lambda i: (i, 0)),
        out_shape=jax.ShapeDtypeStruct((m, d), F32),
        compiler_params=_cparams(("parallel",)),
        name="out_proj",
    )(x2, og, proj, od, out_norm_w, subln_w, w_out, w_out)


def _mlp_kernel(x_ref, nw_ref, wu_ref, wd_ref, o_ref, n_ref):
    j = pl.program_id(1)
    tm = x_ref.shape[0]

    def ff(n):
        h = jnp.dot(n, wu_ref[...], preferred_element_type=F32)
        h = jnp.square(jnp.maximum(h, 0.0)).astype(BF16)
        return jnp.dot(h, wd_ref[...], preferred_element_type=F32)

    @pl.when(j == 0)
    def _():
        for r0 in range(0, tm, ROW_CHUNK):
            x = x_ref[r0:r0 + ROW_CHUNK, :]
            ms = jnp.mean(x * x, axis=-1, keepdims=True)
            n = (x * lax.rsqrt(ms + NORM_EPS) * nw_ref[...]).astype(BF16)
            n_ref[r0:r0 + ROW_CHUNK, :] = n
            o_ref[r0:r0 + ROW_CHUNK, :] = x + ff(n)

    @pl.when(j > 0)
    def _():
        for r0 in range(0, tm, ROW_CHUNK):
            o_ref[r0:r0 + ROW_CHUNK, :] += ff(n_ref[r0:r0 + ROW_CHUNK, :])


def _mlp(x2, norm_w, w_up, w_down, layer, tm, tf):
    m, d = x2.shape
    f = w_up.shape[2]
    return pl.pallas_call(
        _mlp_kernel,
        grid=(m // tm, f // tf),
        in_specs=[
            pl.BlockSpec((tm, d), lambda i, j: (i, 0)),
            pl.BlockSpec((1, d), lambda i, j: (0, 0)),
            pl.BlockSpec((None, d, tf), lambda i, j: (layer, 0, j)),
            pl.BlockSpec((None, tf, d), lambda i, j: (layer, j, 0)),
        ],
        out_specs=pl.BlockSpec((tm, d), lambda i, j: (i, 0)),
        out_shape=jax.ShapeDtypeStruct((m, d), F32),
        scratch_shapes=[pltpu.VMEM((tm, d), BF16)],
        compiler_params=_cparams(("parallel", "arbitrary")),
        name="mlp",
    )(x2, norm_w, w_up, w_down)


def _lambda_init(layer_idx):
    return 0.8 - 0.6 * math.exp(-0.3 * layer_idx)


def kernel(x, positions, attn_norm_w, w_in, gla_gate_w2, gla_gate_b, gla_out_norm_w,
           diff_q_norm_w, diff_k_norm_w, diff_lambda_q1, diff_lambda_k1, diff_lambda_q2,
           diff_lambda_k2, diff_subln_w, w_out, mlp_norm_w, w_up, w_down):
    batch, seq, d = x.shape
    depth = w_in.shape[0]
    m = batch * seq
    assert d == GLA_WIDTH + DIFF_WIDTH and seq % GLA_CHUNK == 0 and seq % Q_TILE == 0
    tm_proj = min(512, m)
    tm_out = min(512, m)
    tm_mlp = min(512, m)

    half = DIFF_HEAD_DIM // 2
    inv_freq = ROPE_THETA ** (-jnp.arange(0, DIFF_HEAD_DIM, 2, dtype=F32) / DIFF_HEAD_DIM)
    ang = (positions.astype(F32)[..., None] * inv_freq).reshape(m, half)
    cos, sin = jnp.cos(ang), jnp.sin(ang)

    wstack_np, level_np = _gla_constants()
    wstack = jnp.asarray(wstack_np, dtype=BF16)
    level = jnp.asarray(level_np)

    sizes = (GLA_KEY_WIDTH, GLA_KEY_WIDTH, GLA_WIDTH, GLA_WIDTH, GLA_GATE_RANK,
             DIFF_WIDTH, DIFF_WIDTH, DIFF_WIDTH)
    offs = np.concatenate([[0], np.cumsum(sizes)])
    lr0, lr1 = int(offs[4]), int(offs[5])

    w_a = w_in[:, :, :lr0].astype(BF16)
    w_b = w_in[:, :, lr1:].astype(BF16)
    w_lr = jnp.pad(w_in[:, :, lr0:lr1],
                   ((0, 0), (0, 0), (0, V7X_LANES - GLA_GATE_RANK))).astype(BF16)
    w_out_b = w_out.astype(BF16)
    w_up_b = w_up.astype(BF16)
    w_down_b = w_down.astype(BF16)

    x2 = x.reshape(m, d)
    for layer in range(depth):
        w2p = jnp.pad(gla_gate_w2[layer], ((0, V7X_LANES - GLA_GATE_RANK), (0, 0))).astype(BF16)
        gate_b = gla_gate_b[layer][None, :]

        qnw = diff_q_norm_w[layer][None, :]
        knw = diff_k_norm_w[layer][None, :]
        proj, lr, b2 = _in_proj(x2, attn_norm_w[layer][None, :], w_a, w_b, w_lr, cos, sin,
                                qnw, jnp.roll(qnw, half, axis=1), knw, jnp.roll(knw, half, axis=1),
                                w2p, gate_b, layer, tm_proj)
        o_gla = _gla(proj, b2, lr, w2p, gate_b, wstack, level, batch, seq)
        o_diff = _diff_attn(proj, qnw, knw,
                            diff_lambda_q1[layer][None, :], diff_lambda_k1[layer][None, :],
                            diff_lambda_q2[layer][None, :], diff_lambda_k2[layer][None, :],
                            batch, seq, _lambda_init(layer))
        x2 = _out_proj(x2, o_gla, proj, o_diff, gla_out_norm_w[layer][None, :],
                       diff_subln_w[layer][None, :], w_out_b, layer, tm_out, _lambda_init(layer))
        x2 = _mlp(x2, mlp_norm_w[layer][None, :], w_up_b, w_down_b, layer, tm_mlp, 2048)
    return x2.reshape(batch, seq, d)
```

```python
import functools
import math

import numpy as np
import jax
import jax.numpy as jnp
from jax import lax
from jax.experimental import pallas as pl
from jax.experimental.pallas import tpu as pltpu

F32 = jnp.float32
BF16 = jnp.bfloat16

GLA_HEADS = 4
GLA_HEAD_K = 128
GLA_HEAD_V = 256
GLA_KEY_WIDTH = GLA_HEADS * GLA_HEAD_K
GLA_WIDTH = GLA_HEADS * GLA_HEAD_V
GLA_GATE_RANK = 16
GLA_GATE_TAU = 16.0
DIFF_HEADS = 4
DIFF_HEAD_DIM = 128
DIFF_WIDTH = DIFF_HEADS * 2 * DIFF_HEAD_DIM
ROPE_THETA = 10000.0
NORM_EPS = 1e-6
SUBLN_EPS = 1e-5
LOG2E = math.log2(math.e)
DIFF_QSCALE = (DIFF_HEAD_DIM ** -0.5) * LOG2E

V7X_LANES = 128
V7X_VMEM_BYTES = 64 * 1024 * 1024
VMEM_LIMIT_BYTES = 56 * 1024 * 1024

COL_GQ, COL_GK, COL_GV, COL_GG = 0, 512, 1024, 2048
COL_DQ, COL_DK, COL_DV = 3072, 4096, 5120
MAIN_COLS = 6144

GLA_CHUNK = 256
GLA_LEVELS = int(math.log2(GLA_CHUNK))
GLA_FAST_MAX_DECAY = 64.0
Q_TILE = 512
ROW_CHUNK = 256
SCORE_BOUND_NO_SHIFT = 60.0


def _cparams(sem):
    return pltpu.CompilerParams(dimension_semantics=sem, vmem_limit_bytes=VMEM_LIMIT_BYTES)


def _log_decay(z):
    return (jnp.minimum(z, 0.0) - jnp.log1p(jnp.exp(-jnp.abs(z)))) * (1.0 / GLA_GATE_TAU)


def _log2_decay(z):
    soft = jnp.log2(1.0 + jnp.exp2(jnp.abs(z) * (-LOG2E)))
    return jnp.minimum(z, 0.0) * (LOG2E / GLA_GATE_TAU) - soft * (1.0 / GLA_GATE_TAU)


def _log_decay_split(z):
    la = _log_decay(z)
    la_hi = la.astype(BF16)
    la_lo = (la - la_hi.astype(F32)).astype(BF16)
    return jnp.concatenate([la_hi, la_lo], axis=1)


def _cumsum_rows(x):
    n = x.shape[0]
    row = lax.broadcasted_iota(jnp.int32, (n, 1), 0)
    s = 1
    while s < n:
        x = x + jnp.where(row >= s, pltpu.roll(x, s, 0), 0.0)
        s *= 2
    return x


def _in_proj_kernel(x_ref, nw_ref, wa_ref, wb_ref, wlr_ref, cos_ref, sin_ref,
                    qnw_ref, qnr_ref, knw_ref, knr_ref, w2_ref, gb_ref,
                    o_ref, olr_ref, b2_ref):
    tm = x_ref.shape[0]
    na = wa_ref.shape[1]
    dh = DIFF_HEAD_DIM
    n_half_heads = DIFF_WIDTH // dh
    for r0 in range(0, tm, ROW_CHUNK):
        rows = slice(r0, r0 + ROW_CHUNK)
        x = x_ref[rows, :]
        ms = jnp.mean(x * x, axis=-1, keepdims=True)
        n = (x * lax.rsqrt(ms + NORM_EPS) * nw_ref[...]).astype(BF16)
        lr = jnp.dot(n, wlr_ref[...], preferred_element_type=F32).astype(BF16)
        olr_ref[rows, :] = lr
        z = jnp.dot(lr, w2_ref[...], preferred_element_type=F32) + gb_ref[...]
        for h in range(GLA_HEADS):
            cols = slice(h * GLA_HEAD_K, (h + 1) * GLA_HEAD_K)
            b2_ref[rows, cols] = _cumsum_rows(_log2_decay(z[:, cols]))
        o_ref[rows, 0:na] = jnp.dot(n, wa_ref[...], preferred_element_type=F32).astype(BF16)
        ob = jnp.dot(n, wb_ref[...], preferred_element_type=F32)

        cosf = jnp.concatenate([cos_ref[rows, :], cos_ref[rows, :]], axis=1)
        sinf = jnp.concatenate([-sin_ref[rows, :], sin_ref[rows, :]], axis=1)
        tabs = ((cosf * (qnw_ref[...] * DIFF_QSCALE), sinf * (qnr_ref[...] * DIFF_QSCALE)),
                (cosf * knw_ref[...], sinf * knr_ref[...]))
        for i in range(2 * n_half_heads):
            ta, tb = tabs[i // n_half_heads]
            t = ob[:, i * dh:(i + 1) * dh]
            r = lax.rsqrt(jnp.mean(t * t, axis=-1, keepdims=True) + NORM_EPS)
            o_ref[rows, na + i * dh:na + (i + 1) * dh] = (
                r * (t * ta + pltpu.roll(t, dh // 2, 1) * tb)).astype(BF16)
        o_ref[rows, na + 2 * DIFF_WIDTH:] = ob[:, 2 * DIFF_WIDTH:].astype(BF16)


def _in_proj(x2, norm_w, w_a, w_b, w_lr, cos, sin, qnw, qnr, knw, knr, w2p, gate_b, layer, tm):
    assert ROW_CHUNK == GLA_CHUNK
    m, d = x2.shape
    resident = pl.Buffered(1)
    half = DIFF_HEAD_DIM // 2
    vec = pl.BlockSpec((1, DIFF_HEAD_DIM), lambda i: (0, 0))
    return pl.pallas_call(
        _in_proj_kernel,
        grid=(m // tm,),
        in_specs=[
            pl.BlockSpec((tm, d), lambda i: (i, 0)),
            pl.BlockSpec((1, d), lambda i: (0, 0)),
            pl.BlockSpec((None, d, w_a.shape[2]), lambda i: (layer, 0, 0), pipeline_mode=resident),
            pl.BlockSpec((None, d, w_b.shape[2]), lambda i: (layer, 0, 0), pipeline_mode=resident),
            pl.BlockSpec((None, d, V7X_LANES), lambda i: (layer, 0, 0), pipeline_mode=resident),
            pl.BlockSpec((tm, half), lambda i: (i, 0)),
            pl.BlockSpec((tm, half), lambda i: (i, 0)),
            vec, vec, vec, vec,
            pl.BlockSpec((V7X_LANES, GLA_KEY_WIDTH), lambda i: (0, 0)),
            pl.BlockSpec((1, GLA_KEY_WIDTH), lambda i: (0, 0)),
        ],
        out_specs=[
            pl.BlockSpec((tm, MAIN_COLS), lambda i: (i, 0)),
            pl.BlockSpec((tm, V7X_LANES), lambda i: (i, 0)),
            pl.BlockSpec((tm, GLA_KEY_WIDTH), lambda i: (i, 0)),
        ],
        out_shape=[
            jax.ShapeDtypeStruct((m, MAIN_COLS), BF16),
            jax.ShapeDtypeStruct((m, V7X_LANES), BF16),
            jax.ShapeDtypeStruct((m, GLA_KEY_WIDTH), F32),
        ],
        compiler_params=_cparams(("parallel",)),
        name="in_proj",
    )(x2, norm_w, w_a, w_b, w_lr, cos, sin, qnw, qnr, knw, knr, w2p, gate_b)


def _gla_constants():
    c = GLA_CHUNK
    i = np.arange(c)[:, None]
    t = np.arange(c)[None, :]
    mats = [(t <= i), (t > i)]
    for lvl in range(GLA_LEVELS):
        s = 1 << lvl
        m = (i // (2 * s)) * (2 * s) + s - 1
        mats.append((t > np.minimum(i, m)) & (t <= np.maximum(i, m)))
    wstack = np.concatenate(mats, axis=0).astype(np.float32)
    x = i ^ t
    level = np.where(i > t, np.floor(np.log2(np.maximum(x, 1))), np.where(i == t, -1, -2))
    return wstack, level.astype(np.int32)


def _gla_kernel(q_ref, k_ref, v_ref, b2_ref, lr_ref, w2_ref, gb_ref, wstk_ref, lvl_ref,
                o_ref, state_ref, *, seq):
    c = GLA_CHUNK
    dk = GLA_HEAD_K
    n_chunks = seq // c
    w2 = w2_ref[...]
    gb = gb_ref[...]
    eye = (lax.broadcasted_iota(jnp.int32, (dk, dk), 0)
           == lax.broadcasted_iota(jnp.int32, (dk, dk), 1))
    nt = (((1,), (1,)), ((), ()))
    tn = (((0,), (0,)), ((), ()))

    def log_decay(r0):
        z = jnp.dot(lr_ref[pl.ds(r0, c), :], w2, preferred_element_type=F32) + gb
        return _log_decay_split(z)

    def finish(r0, o, kd, vb, e_last, state):
        o_ref[pl.ds(r0, c), :] = o.astype(o_ref.dtype)
        e_col = jnp.sum(jnp.where(eye, e_last, 0.0), axis=1, keepdims=True)
        state_ref[...] = state * e_col + lax.dot_general(kd, vb, tn, preferred_element_type=F32)

    tot = jnp.zeros((1, dk), F32)
    for ci in range(n_chunks):
        tot = jnp.maximum(tot, -b2_ref[(ci + 1) * c - 1:(ci + 1) * c, :])
    bounded = jnp.max(tot, axis=-1, keepdims=True)[0, 0] <= GLA_FAST_MAX_DECAY
    state_ref[...] = jnp.zeros_like(state_ref)

    def fast_chunk(ci, carry):
        r0 = pl.multiple_of(ci * c, c)
        qf = q_ref[pl.ds(r0, c), :].astype(F32)
        kf = k_ref[pl.ds(r0, c), :].astype(F32)
        vb = v_ref[pl.ds(r0, c), :]
        b2 = b2_ref[pl.ds(r0, c), :]
        eq = jnp.exp2(b2)
        qt = (qf * eq).astype(BF16)
        kt = kf * jnp.exp2(-b2)
        state = state_ref[...]
        causal = (lax.broadcasted_iota(jnp.int32, (c, c), 0)
                  >= lax.broadcasted_iota(jnp.int32, (c, c), 1))
        p = lax.dot_general(qt, kt.astype(BF16), nt, preferred_element_type=F32)
        o = jnp.dot(qt, state.astype(BF16), preferred_element_type=F32)
        o = o + jnp.dot(jnp.where(causal, p, 0.0).astype(BF16), vb, preferred_element_type=F32)
        e_last = eq[c - 1:c, :]
        finish(r0, o, (kt * e_last).astype(BF16), vb, e_last, state)
        return carry

    def level_chunk(ci, carry):
        r0 = pl.multiple_of(ci * c, c)
        lvl = lvl_ref[...]
        row = lax.broadcasted_iota(jnp.int32, (c, 1), 0)
        qf = q_ref[pl.ds(r0, c), :].astype(F32)
        kf = k_ref[pl.ds(r0, c), :].astype(F32)
        kb = k_ref[pl.ds(r0, c), :]
        vb = v_ref[pl.ds(r0, c), :]
        sums = jnp.dot(wstk_ref[...], log_decay(r0), preferred_element_type=F32)
        sums = sums[:, :dk] + sums[:, dk:]
        b = sums[0:c]
        b_rev = sums[c:2 * c]
        state = state_ref[...]
        o = jnp.dot((qf * jnp.exp(b)).astype(BF16), state.astype(BF16),
                    preferred_element_type=F32)
        attn = jnp.where(lvl == -1,
                         lax.dot_general(qf.astype(BF16), kb, nt, preferred_element_type=F32), 0.0)
        for l in range(GLA_LEVELS):
            e = jnp.exp(sums[(l + 2) * c:(l + 3) * c])
            upper = ((row >> l) & 1) == 1
            qt = jnp.where(upper, qf * e, 0.0).astype(BF16)
            kt = jnp.where(upper, 0.0, kf * e).astype(BF16)
            p = lax.dot_general(qt, kt, nt, preferred_element_type=F32)
            attn = jnp.where(lvl == l, p, attn)
        o = o + jnp.dot(attn.astype(BF16), vb, preferred_element_type=F32)
        finish(r0, o, (kf * jnp.exp(b_rev)).astype(BF16), vb, jnp.exp(b[c - 1:c, :]), state)
        return carry

    @pl.when(bounded)
    def _():
        lax.fori_loop(0, n_chunks, fast_chunk, 0, unroll=8)

    @pl.when(jnp.logical_not(bounded))
    def _():
        lax.fori_loop(0, n_chunks, level_chunk, 0)


def _gla(proj, b2, lr, w2p, gate_b, wstack, level, batch, seq):
    m = proj.shape[0]
    kq = GLA_HEAD_K
    kv = GLA_HEAD_V
    rows = wstack.shape[0]
    return pl.pallas_call(
        functools.partial(_gla_kernel, seq=seq),
        grid=(batch, GLA_HEADS),
        in_specs=[
            pl.BlockSpec((seq, kq), lambda b, h: (b, COL_GQ // kq + h)),
            pl.BlockSpec((seq, kq), lambda b, h: (b, COL_GK // kq + h)),
            pl.BlockSpec((seq, kv), lambda b, h: (b, COL_GV // kv + h)),
            pl.BlockSpec((seq, kq), lambda b, h: (b, h)),
            pl.BlockSpec((seq, V7X_LANES), lambda b, h: (b, 0)),
            pl.BlockSpec((V7X_LANES, kq), lambda b, h: (0, h)),
            pl.BlockSpec((1, kq), lambda b, h: (0, h)),
            pl.BlockSpec((rows, GLA_CHUNK), lambda b, h: (0, 0)),
            pl.BlockSpec((GLA_CHUNK, GLA_CHUNK), lambda b, h: (0, 0)),
        ],
        out_specs=pl.BlockSpec((seq, kv), lambda b, h: (b, h)),
        out_shape=jax.ShapeDtypeStruct((m, GLA_WIDTH), BF16),
        scratch_shapes=[pltpu.VMEM((kq, kv), F32)],
        compiler_params=_cparams(("parallel", "parallel")),
        name="gla",
    )(proj, proj, proj, b2, lr, w2p, gate_b, wstack, level)


def _diff_kernel(q_ref, k_ref, v_ref, qnw_ref, knw_ref, lq1_ref, lk1_ref, lq2_ref, lk2_ref,
                 o_ref, *, seq, lam_init):
    dh = DIFF_HEAD_DIM
    tq = Q_TILE
    lam = (jnp.exp(jnp.sum(lq1_ref[...] * lk1_ref[...], axis=-1, keepdims=True))
           - jnp.exp(jnp.sum(lq2_ref[...] * lk2_ref[...], axis=-1, keepdims=True))
           + lam_init)

    nt = (((1,), (1,)), ((), ()))
    tri = (lax.broadcasted_iota(jnp.int32, (tq, tq), 0)
           >= lax.broadcasted_iota(jnp.int32, (tq, tq), 1))

    def attend(subtract_max):
        for t in reversed(range(seq // tq)):
            r0 = t * tq
            ps, ls = [], []
            for comp in range(2):
                cols = slice(comp * dh, (comp + 1) * dh)
                q = q_ref[r0:r0 + tq, cols]
                s = lax.dot_general(q, k_ref[r0:r0 + tq, cols], nt, preferred_element_type=F32)
                s = jnp.where(tri, s, -jnp.inf)
                if t > 0:
                    s_off = lax.dot_general(q, k_ref[0:r0, cols], nt,
                                            preferred_element_type=F32)
                    s = jnp.concatenate([s_off, s], axis=1)
                if subtract_max:
                    s = s - jnp.max(s, axis=-1, keepdims=True)
                p = jnp.exp2(s)
                ls.append(jnp.sum(p, axis=-1, keepdims=True))
                ps.append(p.astype(BF16))
            ratio = (lam * ls[0] / ls[1]).astype(BF16)
            w = ps[0] - ratio * ps[1]
            o = jnp.dot(w, v_ref[0:r0 + tq, :], preferred_element_type=F32) * (1.0 / ls[0])
            o_ref[r0:r0 + tq, :] = o.astype(o_ref.dtype)

    bound = (dh * DIFF_QSCALE) * (jnp.max(jnp.abs(qnw_ref[...]), axis=-1, keepdims=True)
                             * jnp.max(jnp.abs(knw_ref[...]), axis=-1, keepdims=True))
    small = bound[0, 0] <= SCORE_BOUND_NO_SHIFT

    @pl.when(small)
    def _():
        attend(False)

    @pl.when(jnp.logical_not(small))
    def _():
        attend(True)


def _diff_attn(proj, qnw, knw, lq1, lk1, lq2, lk2, batch, seq, lam_init):
    m = proj.shape[0]
    hw = 2 * DIFF_HEAD_DIM
    vec = pl.BlockSpec((1, DIFF_HEAD_DIM), lambda b, h: (0, 0))
    return pl.pallas_call(
        functools.partial(_diff_kernel, seq=seq, lam_init=lam_init),
        grid=(batch, DIFF_HEADS),
        in_specs=[
            pl.BlockSpec((seq, hw), lambda b, h: (b, COL_DQ // hw + h)),
            pl.BlockSpec((seq, hw), lambda b, h: (b, COL_DK // hw + h)),
            pl.BlockSpec((seq, hw), lambda b, h: (b, COL_DV // hw + h)),
            vec, vec, vec, vec, vec, vec,
        ],
        out_specs=pl.BlockSpec((seq, hw), lambda b, h: (b, h)),
        out_shape=jax.ShapeDtypeStruct((m, DIFF_WIDTH), BF16),
        compiler_params=_cparams(("parallel", "parallel")),
        name="diff_attn",
    )(proj, proj, proj, qnw, knw, lq1, lk1, lq2, lk2)


def _out_proj_kernel(x_ref, og_ref, g_ref, od_ref, onw_ref, sw_ref, wg_ref, wd_ref, o_ref,
                     *, lam_init):
    tm = x_ref.shape[0]
    hv = GLA_HEAD_V
    hd = 2 * DIFF_HEAD_DIM
    gla_eps = NORM_EPS * GLA_HEAD_K
    onw = onw_ref[...]
    sw = sw_ref[...] * (1.0 - lam_init)
    for r0 in range(0, tm, ROW_CHUNK):
        rows = slice(r0, r0 + ROW_CHUNK)
        ys = []
        for h in range(GLA_HEADS):
            o = og_ref[rows, h * hv:(h + 1) * hv].astype(F32)
            g = g_ref[rows, h * hv:(h + 1) * hv].astype(F32)
            y = o * lax.rsqrt(jnp.mean(o * o, axis=-1, keepdims=True) + gla_eps) * onw
            ys.append((y * (g * jax.nn.sigmoid(g))).astype(BF16))
        acc = jnp.dot(jnp.concatenate(ys, axis=1), wg_ref[...], preferred_element_type=F32)
        ys = []
        for h in range(DIFF_HEADS):
            o = od_ref[rows, h * hd:(h + 1) * hd].astype(F32)
            y = o * lax.rsqrt(jnp.mean(o * o, axis=-1, keepdims=True) + SUBLN_EPS) * sw
            ys.append(y.astype(BF16))
        acc = acc + jnp.dot(jnp.concatenate(ys, axis=1), wd_ref[...],
                            preferred_element_type=F32)
        o_ref[rows, :] = x_ref[rows, :] + acc


def _out_proj(x2, og, proj, od, out_norm_w, subln_w, w_out, layer, tm, lam_init):
    m, d = x2.shape
    return pl.pallas_call(
        functools.partial(_out_proj_kernel, lam_init=lam_init),
        grid=(m // tm,),
        in_specs=[
            pl.BlockSpec((tm, d), lambda i: (i, 0)),
            pl.BlockSpec((tm, GLA_WIDTH), lambda i: (i, 0)),
            pl.BlockSpec((tm, GLA_WIDTH), lambda i: (i, COL_GG // GLA_WIDTH)),
            pl.BlockSpec((tm, DIFF_WIDTH), lambda i: (i, 0)),
            pl.BlockSpec((1, GLA_HEAD_V), lambda i: (0, 0)),
            pl.BlockSpec((1, 2 * DIFF_HEAD_DIM), lambda i: (0, 0)),
            pl.BlockSpec((None, GLA_WIDTH, d), lambda i: (layer, 0, 0)),
            pl.BlockSpec((None, DIFF_WIDTH, d), lambda i: (layer, GLA_WIDTH // DIFF_WIDTH, 0)),
        ],
        out_specs=pl.BlockSpec((tm, d), lambda i: (i, 0)),
        out_shape=jax.ShapeDtypeStruct((m, d), F32),
        compiler_params=_cparams(("parallel",)),
        name="out_proj",
    )(x2, og, proj, od, out_norm_w, subln_w, w_out, w_out)


def _mlp_kernel(x_ref, nw_ref, wu_ref, wd_ref, o_ref, n_ref):
    j = pl.program_id(1)
    tm = x_ref.shape[0]

    def ff(n):
        h = jnp.dot(n, wu_ref[...], preferred_element_type=F32)
        h = jnp.square(jnp.maximum(h, 0.0)).astype(BF16)
        return jnp.dot(h, wd_ref[...], preferred_element_type=F32)

    @pl.when(j == 0)
    def _():
        for r0 in range(0, tm, ROW_CHUNK):
            x = x_ref[r0:r0 + ROW_CHUNK, :]
            ms = jnp.mean(x * x, axis=-1, keepdims=True)
            n = (x * lax.rsqrt(ms + NORM_EPS) * nw_ref[...]).astype(BF16)
            n_ref[r0:r0 + ROW_CHUNK, :] = n
            o_ref[r0:r0 + ROW_CHUNK, :] = x + ff(n)

    @pl.when(j > 0)
    def _():
        for r0 in range(0, tm, ROW_CHUNK):
            o_ref[r0:r0 + ROW_CHUNK, :] += ff(n_ref[r0:r0 + ROW_CHUNK, :])


def _mlp(x2, norm_w, w_up, w_down, layer, tm, tf):
    m, d = x2.shape
    f = w_up.shape[2]
    return pl.pallas_call(
        _mlp_kernel,
        grid=(m // tm, f // tf),
        in_specs=[
            pl.BlockSpec((tm, d), lambda i, j: (i, 0)),
            pl.BlockSpec((1, d), lambda i, j: (0, 0)),
            pl.BlockSpec((None, d, tf), lambda i, j: (layer, 0, j)),
            pl.BlockSpec((None, tf, d), lambda i, j: (layer, j, 0)),
        ],
        out_specs=pl.BlockSpec((tm, d), lambda i, j: (i, 0)),
        out_shape=jax.ShapeDtypeStruct((m, d), F32),
        scratch_shapes=[pltpu.VMEM((tm, d), BF16)],
        compiler_params=_cparams(("parallel", "arbitrary")),
        name="mlp",
    )(x2, norm_w, w_up, w_down)


def _lambda_init(layer_idx):
    return 0.8 - 0.6 * math.exp(-0.3 * layer_idx)


def kernel(x, positions, attn_norm_w, w_in, gla_gate_w2, gla_gate_b, gla_out_norm_w,
           diff_q_norm_w, diff_k_norm_w, diff_lambda_q1, diff_lambda_k1, diff_lambda_q2,
           diff_lambda_k2, diff_subln_w, w_out, mlp_norm_w, w_up, w_down):
    batch, seq, d = x.shape
    depth = w_in.shape[0]
    m = batch * seq
    assert d == GLA_WIDTH + DIFF_WIDTH and seq % GLA_CHUNK == 0 and seq % Q_TILE == 0
    tm_proj = min(512, m)
    tm_out = min(512, m)
    tm_mlp = min(512, m)

    half = DIFF_HEAD_DIM // 2
    inv_freq = ROPE_THETA ** (-jnp.arange(0, DIFF_HEAD_DIM, 2, dtype=F32) / DIFF_HEAD_DIM)
    ang = (positions.astype(F32)[..., None] * inv_freq).reshape(m, half)
    cos, sin = jnp.cos(ang), jnp.sin(ang)

    wstack_np, level_np = _gla_constants()
    wstack = jnp.asarray(wstack_np, dtype=BF16)
    level = jnp.asarray(level_np)

    sizes = (GLA_KEY_WIDTH, GLA_KEY_WIDTH, GLA_WIDTH, GLA_WIDTH, GLA_GATE_RANK,
             DIFF_WIDTH, DIFF_WIDTH, DIFF_WIDTH)
    offs = np.concatenate([[0], np.cumsum(sizes)])
    lr0, lr1 = int(offs[4]), int(offs[5])

    w_a = w_in[:, :, :lr0].astype(BF16)
    w_b = w_in[:, :, lr1:].astype(BF16)
    w_lr = jnp.pad(w_in[:, :, lr0:lr1],
                   ((0, 0), (0, 0), (0, V7X_LANES - GLA_GATE_RANK))).astype(BF16)
    w_out_b = w_out.astype(BF16)
    w_up_b = w_up.astype(BF16)
    w_down_b = w_down.astype(BF16)

    x2 = x.reshape(m, d)
    for layer in range(depth):
        w2p = jnp.pad(gla_gate_w2[layer], ((0, V7X_LANES - GLA_GATE_RANK), (0, 0))).astype(BF16)
        gate_b = gla_gate_b[layer][None, :]

        qnw = diff_q_norm_w[layer][None, :]
        knw = diff_k_norm_w[layer][None, :]
        proj, lr, b2 = _in_proj(x2, attn_norm_w[layer][None, :], w_a, w_b, w_lr, cos, sin,
                                qnw, jnp.roll(qnw, half, axis=1), knw, jnp.roll(knw, half, axis=1),
                                w2p, gate_b, layer, tm_proj)
        o_gla = _gla(proj, b2, lr, w2p, gate_b, wstack, level, batch, seq)
        o_diff = _diff_attn(proj, qnw, knw,
                            diff_lambda_q1[layer][None, :], diff_lambda_k1[layer][None, :],
                            diff_lambda_q2[layer][None, :], diff_lambda_k2[layer][None, :],
                            batch, seq, _lambda_init(layer))
        x2 = _out_proj(x2, o_gla, proj, o_diff, gla_out_norm_w[layer][None, :],
                       diff_subln_w[layer][None, :], w_out_b, layer, tm_out, _lambda_init(layer))
        x2 = _mlp(x2, mlp_norm_w[layer][None, :], w_up_b, w_down_b, layer, tm_mlp, 2048)
    return x2.reshape(batch, seq, d)
```

```python
import functools
import math

import numpy as np
import jax
import jax.numpy as jnp
from jax import lax
from jax.experimental import pallas as pl
from jax.experimental.pallas import tpu as pltpu

F32 = jnp.float32
BF16 = jnp.bfloat16

GLA_HEADS = 4
GLA_HEAD_K = 128
GLA_HEAD_V = 256
GLA_KEY_WIDTH = GLA_HEADS * GLA_HEAD_K
GLA_WIDTH = GLA_HEADS * GLA_HEAD_V
GLA_GATE_RANK = 16
GLA_GATE_TAU = 16.0
DIFF_HEADS = 4
DIFF_HEAD_DIM = 128
DIFF_WIDTH = DIFF_HEADS * 2 * DIFF_HEAD_DIM
ROPE_THETA = 10000.0
NORM_EPS = 1e-6
SUBLN_EPS = 1e-5
LOG2E = math.log2(math.e)
DIFF_QSCALE = (DIFF_HEAD_DIM ** -0.5) * LOG2E

V7X_LANES = 128
V7X_VMEM_BYTES = 64 * 1024 * 1024
VMEM_LIMIT_BYTES = V7X_VMEM_BYTES - 8 * 1024 * 1024

COL_GQ = 0
COL_GK = COL_GQ + GLA_KEY_WIDTH
COL_GV = COL_GK + GLA_KEY_WIDTH
COL_GG = COL_GV + GLA_WIDTH
COL_DQ = COL_GG + GLA_WIDTH
COL_DK = COL_DQ + DIFF_WIDTH
COL_DV = COL_DK + DIFF_WIDTH
MAIN_COLS = COL_DV + DIFF_WIDTH

GLA_CHUNK = 256
GLA_LEVELS = int(math.log2(GLA_CHUNK))
GLA_FAST_MAX_DECAY = 64.0
Q_TILE = 512
ROW_CHUNK = 256
SCORE_BOUND_NO_SHIFT = 60.0


def _cparams(sem):
    return pltpu.CompilerParams(dimension_semantics=sem, vmem_limit_bytes=VMEM_LIMIT_BYTES)


def _log_decay(z):
    return (jnp.minimum(z, 0.0) - jnp.log1p(jnp.exp(-jnp.abs(z)))) * (1.0 / GLA_GATE_TAU)


def _log2_decay(z):
    soft = jnp.log2(1.0 + jnp.exp2(jnp.abs(z) * (-LOG2E)))
    return jnp.minimum(z, 0.0) * (LOG2E / GLA_GATE_TAU) - soft * (1.0 / GLA_GATE_TAU)


def _log_decay_split(z):
    la = _log_decay(z)
    la_hi = la.astype(BF16)
    la_lo = (la - la_hi.astype(F32)).astype(BF16)
    return jnp.concatenate([la_hi, la_lo], axis=1)


def _cumsum_rows(x):
    n = x.shape[0]
    row = lax.broadcasted_iota(jnp.int32, (n, 1), 0)
    s = 1
    while s < n:
        x = x + jnp.where(row >= s, pltpu.roll(x, s, 0), 0.0)
        s *= 2
    return x


def _in_proj_kernel(x_ref, nw_ref, wa_ref, wb_ref, wlr_ref, cos_ref, sin_ref,
                    qnw_ref, qnr_ref, knw_ref, knr_ref, w2_ref, gb_ref,
                    o_ref, olr_ref, b2_ref):
    tm = x_ref.shape[0]
    na = wa_ref.shape[1]
    dh = DIFF_HEAD_DIM
    n_half_heads = DIFF_WIDTH // dh
    for r0 in range(0, tm, ROW_CHUNK):
        rows = slice(r0, r0 + ROW_CHUNK)
        x = x_ref[rows, :]
        ms = jnp.mean(x * x, axis=-1, keepdims=True)
        n = (x * lax.rsqrt(ms + NORM_EPS) * nw_ref[...]).astype(BF16)
        lr = jnp.dot(n, wlr_ref[...], preferred_element_type=F32).astype(BF16)
        olr_ref[rows, :] = lr
        z = jnp.dot(lr, w2_ref[...], preferred_element_type=F32) + gb_ref[...]
        for h in range(GLA_HEADS):
            cols = slice(h * GLA_HEAD_K, (h + 1) * GLA_HEAD_K)
            b2_ref[rows, cols] = _cumsum_rows(_log2_decay(z[:, cols]))
        o_ref[rows, 0:na] = jnp.dot(n, wa_ref[...], preferred_element_type=F32).astype(BF16)
        ob = jnp.dot(n, wb_ref[...], preferred_element_type=F32)

        cosf = jnp.concatenate([cos_ref[rows, :], cos_ref[rows, :]], axis=1)
        sinf = jnp.concatenate([-sin_ref[rows, :], sin_ref[rows, :]], axis=1)
        tabs = ((cosf * (qnw_ref[...] * DIFF_QSCALE), sinf * (qnr_ref[...] * DIFF_QSCALE)),
                (cosf * knw_ref[...], sinf * knr_ref[...]))
        for i in range(2 * n_half_heads):
            ta, tb = tabs[i // n_half_heads]
            t = ob[:, i * dh:(i + 1) * dh]
            r = lax.rsqrt(jnp.mean(t * t, axis=-1, keepdims=True) + NORM_EPS)
            o_ref[rows, na + i * dh:na + (i + 1) * dh] = (
                r * (t * ta + pltpu.roll(t, dh // 2, 1) * tb)).astype(BF16)
        o_ref[rows, na + 2 * DIFF_WIDTH:] = ob[:, 2 * DIFF_WIDTH:].astype(BF16)


def _in_proj(x2, norm_w, w_a, w_b, w_lr, cos, sin, qnw, qnr, knw, knr, w2p, gate_b, layer, tm):
    assert ROW_CHUNK == GLA_CHUNK
    m, d = x2.shape
    resident = pl.Buffered(1)
    half = DIFF_HEAD_DIM // 2
    vec = pl.BlockSpec((1, DIFF_HEAD_DIM), lambda i: (0, 0))
    return pl.pallas_call(
        _in_proj_kernel,
        grid=(m // tm,),
        in_specs=[
            pl.BlockSpec((tm, d), lambda i: (i, 0)),
            pl.BlockSpec((1, d), lambda i: (0, 0)),
            pl.BlockSpec((None, d, w_a.shape[2]), lambda i: (layer, 0, 0), pipeline_mode=resident),
            pl.BlockSpec((None, d, w_b.shape[2]), lambda i: (layer, 0, 0), pipeline_mode=resident),
            pl.BlockSpec((None, d, V7X_LANES), lambda i: (layer, 0, 0), pipeline_mode=resident),
            pl.BlockSpec((tm, half), lambda i: (i, 0)),
            pl.BlockSpec((tm, half), lambda i: (i, 0)),
            vec, vec, vec, vec,
            pl.BlockSpec((V7X_LANES, GLA_KEY_WIDTH), lambda i: (0, 0)),
            pl.BlockSpec((1, GLA_KEY_WIDTH), lambda i: (0, 0)),
        ],
        out_specs=[
            pl.BlockSpec((tm, MAIN_COLS), lambda i: (i, 0)),
            pl.BlockSpec((tm, V7X_LANES), lambda i: (i, 0)),
            pl.BlockSpec((tm, GLA_KEY_WIDTH), lambda i: (i, 0)),
        ],
        out_shape=[
            jax.ShapeDtypeStruct((m, MAIN_COLS), BF16),
            jax.ShapeDtypeStruct((m, V7X_LANES), BF16),
            jax.ShapeDtypeStruct((m, GLA_KEY_WIDTH), F32),
        ],
        compiler_params=_cparams(("parallel",)),
        name="in_proj",
    )(x2, norm_w, w_a, w_b, w_lr, cos, sin, qnw, qnr, knw, knr, w2p, gate_b)


def _gla_constants():
    c = GLA_CHUNK
    i = np.arange(c)[:, None]
    t = np.arange(c)[None, :]
    mats = [(t <= i), (t > i)]
    for lvl in range(GLA_LEVELS):
        s = 1 << lvl
        m = (i // (2 * s)) * (2 * s) + s - 1
        mats.append((t > np.minimum(i, m)) & (t <= np.maximum(i, m)))
    wstack = np.concatenate(mats, axis=0).astype(np.float32)
    x = i ^ t
    level = np.where(i > t, np.floor(np.log2(np.maximum(x, 1))), np.where(i == t, -1, -2))
    return wstack, level.astype(np.int32)


def _gla_kernel(q_ref, k_ref, v_ref, b2_ref, lr_ref, w2_ref, gb_ref, wstk_ref, lvl_ref, wo_ref,
                o_ref, wo_bf_ref, state_ref, *, seq):
    wo_bf_ref[...] = wo_ref[...].astype(BF16)
    c = GLA_CHUNK
    dk = GLA_HEAD_K
    n_chunks = seq // c
    w2 = w2_ref[...]
    gb = gb_ref[...]
    eye = (lax.broadcasted_iota(jnp.int32, (dk, dk), 0)
           == lax.broadcasted_iota(jnp.int32, (dk, dk), 1))
    nt = (((1,), (1,)), ((), ()))
    tn = (((0,), (0,)), ((), ()))

    def log_decay(r0):
        z = jnp.dot(lr_ref[pl.ds(r0, c), :], w2, preferred_element_type=F32) + gb
        return _log_decay_split(z)

    def finish(r0, o, kd, vb, e_last, state):
        o_ref[pl.ds(r0, c), :] = o.astype(o_ref.dtype)
        e_col = jnp.sum(jnp.where(eye, e_last, 0.0), axis=1, keepdims=True)
        state_ref[...] = state * e_col + lax.dot_general(kd, vb, tn, preferred_element_type=F32)

    tot = jnp.zeros((1, dk), F32)
    for ci in range(n_chunks):
        tot = jnp.maximum(tot, -b2_ref[(ci + 1) * c - 1:(ci + 1) * c, :])
    bounded = jnp.max(tot, axis=-1, keepdims=True)[0, 0] <= GLA_FAST_MAX_DECAY
    state_ref[...] = jnp.zeros_like(state_ref)

    def fast_chunk(ci, carry):
        r0 = pl.multiple_of(ci * c, c)
        qf = q_ref[pl.ds(r0, c), :].astype(F32)
        kf = k_ref[pl.ds(r0, c), :].astype(F32)
        vb = v_ref[pl.ds(r0, c), :]
        b2 = b2_ref[pl.ds(r0, c), :]
        eq = jnp.exp2(b2)
        qt = (qf * eq).astype(BF16)
        kt = kf * jnp.exp2(-b2)
        state = state_ref[...]
        causal = (lax.broadcasted_iota(jnp.int32, (c, c), 0)
                  >= lax.broadcasted_iota(jnp.int32, (c, c), 1))
        p = lax.dot_general(qt, kt.astype(BF16), nt, preferred_element_type=F32)
        o = jnp.dot(qt, state.astype(BF16), preferred_element_type=F32)
        o = o + jnp.dot(jnp.where(causal, p, 0.0).astype(BF16), vb, preferred_element_type=F32)
        e_last = eq[c - 1:c, :]
        finish(r0, o, (kt * e_last).astype(BF16), vb, e_last, state)
        return carry

    def level_chunk(ci, carry):
        r0 = pl.multiple_of(ci * c, c)
        lvl = lvl_ref[...]
        row = lax.broadcasted_iota(jnp.int32, (c, 1), 0)
        qf = q_ref[pl.ds(r0, c), :].astype(F32)
        kf = k_ref[pl.ds(r0, c), :].astype(F32)
        kb = k_ref[pl.ds(r0, c), :]
        vb = v_ref[pl.ds(r0, c), :]
        sums = jnp.dot(wstk_ref[...], log_decay(r0), preferred_element_type=F32)
        sums = sums[:, :dk] + sums[:, dk:]
        b = sums[0:c]
        b_rev = sums[c:2 * c]
        state = state_ref[...]
        o = jnp.dot((qf * jnp.exp(b)).astype(BF16), state.astype(BF16),
                    preferred_element_type=F32)
        attn = jnp.where(lvl == -1,
                         lax.dot_general(qf.astype(BF16), kb, nt, preferred_element_type=F32), 0.0)
        for l in range(GLA_LEVELS):
            e = jnp.exp(sums[(l + 2) * c:(l + 3) * c])
            upper = ((row >> l) & 1) == 1
            qt = jnp.where(upper, qf * e, 0.0).astype(BF16)
            kt = jnp.where(upper, 0.0, kf * e).astype(BF16)
            p = lax.dot_general(qt, kt, nt, preferred_element_type=F32)
            attn = jnp.where(lvl == l, p, attn)
        o = o + jnp.dot(attn.astype(BF16), vb, preferred_element_type=F32)
        finish(r0, o, (kf * jnp.exp(b_rev)).astype(BF16), vb, jnp.exp(b[c - 1:c, :]), state)
        return carry

    @pl.when(bounded)
    def _():
        lax.fori_loop(0, n_chunks, fast_chunk, 0, unroll=8)

    @pl.when(jnp.logical_not(bounded))
    def _():
        lax.fori_loop(0, n_chunks, level_chunk, 0)


def _slab_rows(n_rows, steps):
    rows = n_rows // steps
    assert rows * steps == n_rows and rows % 16 == 0
    return rows


def _gla(proj, b2, lr, w2p, gate_b, wstack, level, w_out, layer, batch, seq):
    m = proj.shape[0]
    kq = GLA_HEAD_K
    kv = GLA_HEAD_V
    rows = wstack.shape[0]
    wo_rows, wo_cols = w_out.shape[1:]
    slab = _slab_rows(wo_rows, batch * GLA_HEADS)
    return pl.pallas_call(
        functools.partial(_gla_kernel, seq=seq),
        grid=(batch, GLA_HEADS),
        in_specs=[
            pl.BlockSpec((seq, kq), lambda b, h: (b, COL_GQ // kq + h)),
            pl.BlockSpec((seq, kq), lambda b, h: (b, COL_GK // kq + h)),
            pl.BlockSpec((seq, kv), lambda b, h: (b, COL_GV // kv + h)),
            pl.BlockSpec((seq, kq), lambda b, h: (b, h)),
            pl.BlockSpec((seq, V7X_LANES), lambda b, h: (b, 0)),
            pl.BlockSpec((V7X_LANES, kq), lambda b, h: (0, h)),
            pl.BlockSpec((1, kq), lambda b, h: (0, h)),
            pl.BlockSpec((rows, GLA_CHUNK), lambda b, h: (0, 0)),
            pl.BlockSpec((GLA_CHUNK, GLA_CHUNK), lambda b, h: (0, 0)),
            pl.BlockSpec((None, slab, wo_cols), lambda b, h: (layer, b * GLA_HEADS + h, 0)),
        ],
        out_specs=[
            pl.BlockSpec((seq, kv), lambda b, h: (b, h)),
            pl.BlockSpec((slab, wo_cols), lambda b, h: (b * GLA_HEADS + h, 0)),
        ],
        out_shape=[
            jax.ShapeDtypeStruct((m, GLA_WIDTH), BF16),
            jax.ShapeDtypeStruct((wo_rows, wo_cols), BF16),
        ],
        scratch_shapes=[pltpu.VMEM((kq, kv), F32)],
        compiler_params=_cparams(("parallel", "parallel")),
        name="gla",
    )(proj, proj, proj, b2, lr, w2p, gate_b, wstack, level, w_out)


def _diff_kernel(q_ref, k_ref, v_ref, qnw_ref, knw_ref, lq1_ref, lk1_ref, lq2_ref, lk2_ref,
                 wu_ref, wd_ref, o_ref, wu_bf_ref, wd_bf_ref, *, seq, lam_init):
    wu_bf_ref[...] = wu_ref[...].astype(BF16)
    wd_bf_ref[...] = wd_ref[...].astype(BF16)
    dh = DIFF_HEAD_DIM
    tq = Q_TILE
    lam = (jnp.exp(jnp.sum(lq1_ref[...] * lk1_ref[...], axis=-1, keepdims=True))
           - jnp.exp(jnp.sum(lq2_ref[...] * lk2_ref[...], axis=-1, keepdims=True))
           + lam_init)

    nt = (((1,), (1,)), ((), ()))
    tri = (lax.broadcasted_iota(jnp.int32, (tq, tq), 0)
           >= lax.broadcasted_iota(jnp.int32, (tq, tq), 1))

    def attend(subtract_max):
        for t in reversed(range(seq // tq)):
            r0 = t * tq
            ps, ls = [], []
            for comp in range(2):
                cols = slice(comp * dh, (comp + 1) * dh)
                q = q_ref[r0:r0 + tq, cols]
                s = lax.dot_general(q, k_ref[r0:r0 + tq, cols], nt, preferred_element_type=F32)
                s = jnp.where(tri, s, -jnp.inf)
                if t > 0:
                    s_off = lax.dot_general(q, k_ref[0:r0, cols], nt,
                                            preferred_element_type=F32)
                    s = jnp.concatenate([s_off, s], axis=1)
                if subtract_max:
                    s = s - jnp.max(s, axis=-1, keepdims=True)
                p = jnp.exp2(s)
                ls.append(jnp.sum(p, axis=-1, keepdims=True))
                ps.append(p.astype(BF16))
            ratio = (lam * ls[0] / ls[1]).astype(BF16)
            w = ps[0] - ratio * ps[1]
            o = jnp.dot(w, v_ref[0:r0 + tq, :], preferred_element_type=F32) * (1.0 / ls[0])
            o_ref[r0:r0 + tq, :] = o.astype(o_ref.dtype)

    bound = (dh * DIFF_QSCALE) * (jnp.max(jnp.abs(qnw_ref[...]), axis=-1, keepdims=True)
                             * jnp.max(jnp.abs(knw_ref[...]), axis=-1, keepdims=True))
    small = bound[0, 0] <= SCORE_BOUND_NO_SHIFT

    @pl.when(small)
    def _():
        attend(False)

    @pl.when(jnp.logical_not(small))
    def _():
        attend(True)


def _diff_attn(proj, qnw, knw, lq1, lk1, lq2, lk2, w_up, w_down, layer, batch, seq, lam_init):
    m = proj.shape[0]
    hw = 2 * DIFF_HEAD_DIM
    vec = pl.BlockSpec((1, DIFF_HEAD_DIM), lambda b, h: (0, 0))
    steps = batch * DIFF_HEADS
    up_rows, up_cols = w_up.shape[1:]
    dn_rows, dn_cols = w_down.shape[1:]
    up_slab = _slab_rows(up_rows, steps)
    dn_slab = _slab_rows(dn_rows, steps)
    return pl.pallas_call(
        functools.partial(_diff_kernel, seq=seq, lam_init=lam_init),
        grid=(batch, DIFF_HEADS),
        in_specs=[
            pl.BlockSpec((seq, hw), lambda b, h: (b, COL_DQ // hw + h)),
            pl.BlockSpec((seq, hw), lambda b, h: (b, COL_DK // hw + h)),
            pl.BlockSpec((seq, hw), lambda b, h: (b, COL_DV // hw + h)),
            vec, vec, vec, vec, vec, vec,
            pl.BlockSpec((None, up_slab, up_cols), lambda b, h: (layer, b * DIFF_HEADS + h, 0)),
            pl.BlockSpec((None, dn_slab, dn_cols), lambda b, h: (layer, b * DIFF_HEADS + h, 0)),
        ],
        out_specs=[
            pl.BlockSpec((seq, hw), lambda b, h: (b, h)),
            pl.BlockSpec((up_slab, up_cols), lambda b, h: (b * DIFF_HEADS + h, 0)),
            pl.BlockSpec((dn_slab, dn_cols), lambda b, h: (b * DIFF_HEADS + h, 0)),
        ],
        out_shape=[
            jax.ShapeDtypeStruct((m, DIFF_WIDTH), BF16),
            jax.ShapeDtypeStruct((up_rows, up_cols), BF16),
            jax.ShapeDtypeStruct((dn_rows, dn_cols), BF16),
        ],
        compiler_params=_cparams(("parallel", "parallel")),
        name="diff_attn",
    )(proj, proj, proj, qnw, knw, lq1, lk1, lq2, lk2, w_up, w_down)


def _out_proj_kernel(x_ref, og_ref, g_ref, od_ref, onw_ref, sw_ref, wg_ref, wd_ref, o_ref,
                     *, lam_init):
    tm = x_ref.shape[0]
    hv = GLA_HEAD_V
    hd = 2 * DIFF_HEAD_DIM
    gla_eps = NORM_EPS * GLA_HEAD_K
    onw = onw_ref[...]
    sw = sw_ref[...] * (1.0 - lam_init)
    for r0 in range(0, tm, ROW_CHUNK):
        rows = slice(r0, r0 + ROW_CHUNK)
        ys = []
        for h in range(GLA_HEADS):
            o = og_ref[rows, h * hv:(h + 1) * hv].astype(F32)
            g = g_ref[rows, h * hv:(h + 1) * hv].astype(F32)
            y = o * lax.rsqrt(jnp.mean(o * o, axis=-1, keepdims=True) + gla_eps) * onw
            ys.append((y * (g * jax.nn.sigmoid(g))).astype(BF16))
        acc = jnp.dot(jnp.concatenate(ys, axis=1), wg_ref[...], preferred_element_type=F32)
        ys = []
        for h in range(DIFF_HEADS):
            o = od_ref[rows, h * hd:(h + 1) * hd].astype(F32)
            y = o * lax.rsqrt(jnp.mean(o * o, axis=-1, keepdims=True) + SUBLN_EPS) * sw
            ys.append(y.astype(BF16))
        acc = acc + jnp.dot(jnp.concatenate(ys, axis=1), wd_ref[...],
                            preferred_element_type=F32)
        o_ref[rows, :] = x_ref[rows, :] + acc


def _out_proj(x2, og, proj, od, out_norm_w, subln_w, w_out, tm, lam_init):
    m, d = x2.shape
    return pl.pallas_call(
        functools.partial(_out_proj_kernel, lam_init=lam_init),
        grid=(m // tm,),
        in_specs=[
            pl.BlockSpec((tm, d), lambda i: (i, 0)),
            pl.BlockSpec((tm, GLA_WIDTH), lambda i: (i, 0)),
            pl.BlockSpec((tm, GLA_WIDTH), lambda i: (i, COL_GG // GLA_WIDTH)),
            pl.BlockSpec((tm, DIFF_WIDTH), lambda i: (i, 0)),
            pl.BlockSpec((1, GLA_HEAD_V), lambda i: (0, 0)),
            pl.BlockSpec((1, 2 * DIFF_HEAD_DIM), lambda i: (0, 0)),
            pl.BlockSpec((GLA_WIDTH, d), lambda i: (0, 0)),
            pl.BlockSpec((DIFF_WIDTH, d), lambda i: (GLA_WIDTH // DIFF_WIDTH, 0)),
        ],
        out_specs=pl.BlockSpec((tm, d), lambda i: (i, 0)),
        out_shape=jax.ShapeDtypeStruct((m, d), F32),
        compiler_params=_cparams(("parallel",)),
        name="out_proj",
    )(x2, og, proj, od, out_norm_w, subln_w, w_out, w_out)


def _mlp_kernel(x_ref, nw_ref, wu_ref, wd_ref, o_ref, n_ref):
    j = pl.program_id(1)
    tm = x_ref.shape[0]

    def ff(n):
        h = jnp.dot(n, wu_ref[...], preferred_element_type=F32)
        h = jnp.square(jnp.maximum(h, 0.0)).astype(BF16)
        return jnp.dot(h, wd_ref[...], preferred_element_type=F32)

    @pl.when(j == 0)
    def _():
        for r0 in range(0, tm, ROW_CHUNK):
            x = x_ref[r0:r0 + ROW_CHUNK, :]
            ms = jnp.mean(x * x, axis=-1, keepdims=True)
            n = (x * lax.rsqrt(ms + NORM_EPS) * nw_ref[...]).astype(BF16)
            n_ref[r0:r0 + ROW_CHUNK, :] = n
            o_ref[r0:r0 + ROW_CHUNK, :] = x + ff(n)

    @pl.when(j > 0)
    def _():
        for r0 in range(0, tm, ROW_CHUNK):
            o_ref[r0:r0 + ROW_CHUNK, :] += ff(n_ref[r0:r0 + ROW_CHUNK, :])


def _mlp(x2, norm_w, w_up, w_down, tm, tf):
    m, d = x2.shape
    f = w_up.shape[1]
    return pl.pallas_call(
        _mlp_kernel,
        grid=(m // tm, f // tf),
        in_specs=[
            pl.BlockSpec((tm, d), lambda i, j: (i, 0)),
            pl.BlockSpec((1, d), lambda i, j: (0, 0)),
            pl.BlockSpec((d, tf), lambda i, j: (0, j)),
            pl.BlockSpec((tf, d), lambda i, j: (j, 0)),
        ],
        out_specs=pl.BlockSpec((tm, d), lambda i, j: (i, 0)),
        out_shape=jax.ShapeDtypeStruct((m, d), F32),
        scratch_shapes=[pltpu.VMEM((tm, d), BF16)],
        compiler_params=_cparams(("parallel", "arbitrary")),
        name="mlp",
    )(x2, norm_w, w_up, w_down)


def _lambda_init(layer_idx):
    return 0.8 - 0.6 * math.exp(-0.3 * layer_idx)


def kernel(x, positions, attn_norm_w, w_in, gla_gate_w2, gla_gate_b, gla_out_norm_w,
           diff_q_norm_w, diff_k_norm_w, diff_lambda_q1, diff_lambda_k1, diff_lambda_q2,
           diff_lambda_k2, diff_subln_w, w_out, mlp_norm_w, w_up, w_down):
    batch, seq, d = x.shape
    depth = w_in.shape[0]
    m = batch * seq
    assert d == GLA_WIDTH + DIFF_WIDTH and seq % GLA_CHUNK == 0 and seq % Q_TILE == 0
    tm_proj = min(512, m)
    tm_out = min(512, m)
    tm_mlp = min(512, m)

    half = DIFF_HEAD_DIM // 2
    inv_freq = ROPE_THETA ** (-jnp.arange(0, DIFF_HEAD_DIM, 2, dtype=F32) / DIFF_HEAD_DIM)
    ang = (positions.astype(F32)[..., None] * inv_freq).reshape(m, half)
    cos, sin = jnp.cos(ang), jnp.sin(ang)

    wstack_np, level_np = _gla_constants()
    wstack = jnp.asarray(wstack_np, dtype=BF16)
    level = jnp.asarray(level_np)

    sizes = (GLA_KEY_WIDTH, GLA_KEY_WIDTH, GLA_WIDTH, GLA_WIDTH, GLA_GATE_RANK,
             DIFF_WIDTH, DIFF_WIDTH, DIFF_WIDTH)
    offs = np.concatenate([[0], np.cumsum(sizes)])
    lr0, lr1 = int(offs[4]), int(offs[5])

    w_a = w_in[:, :, :lr0].astype(BF16)
    w_b = w_in[:, :, lr1:].astype(BF16)
    w_lr = jnp.pad(w_in[:, :, lr0:lr1],
                   ((0, 0), (0, 0), (0, V7X_LANES - GLA_GATE_RANK))).astype(BF16)

    x2 = x.reshape(m, d)
    for layer in range(depth):
        w2p = jnp.pad(gla_gate_w2[layer], ((0, V7X_LANES - GLA_GATE_RANK), (0, 0))).astype(BF16)
        gate_b = gla_gate_b[layer][None, :]

        qnw = diff_q_norm_w[layer][None, :]
        knw = diff_k_norm_w[layer][None, :]
        proj, lr, b2 = _in_proj(x2, attn_norm_w[layer][None, :], w_a, w_b, w_lr, cos, sin,
                                qnw, jnp.roll(qnw, half, axis=1), knw, jnp.roll(knw, half, axis=1),
                                w2p, gate_b, layer, tm_proj)
        o_gla, w_out_b = _gla(proj, b2, lr, w2p, gate_b, wstack, level, w_out, layer, batch, seq)
        o_diff, w_up_b, w_down_b = _diff_attn(
            proj, qnw, knw,
            diff_lambda_q1[layer][None, :], diff_lambda_k1[layer][None, :],
            diff_lambda_q2[layer][None, :], diff_lambda_k2[layer][None, :],
            w_up, w_down, layer, batch, seq, _lambda_init(layer))
        x2 = _out_proj(x2, o_gla, proj, o_diff, gla_out_norm_w[layer][None, :],
                       diff_subln_w[layer][None, :], w_out_b, tm_out, _lambda_init(layer))
        x2 = _mlp(x2, mlp_norm_w[layer][None, :], w_up_b, w_down_b, tm_mlp, 2048)
    return x2.reshape(batch, seq, d)
```

```python
import functools
import math

import numpy as np
import jax
import jax.numpy as jnp
from jax import lax
from jax.experimental import pallas as pl
from jax.experimental.pallas import tpu as pltpu

F32 = jnp.float32
BF16 = jnp.bfloat16

GLA_HEADS = 4
GLA_HEAD_K = 128
GLA_HEAD_V = 256
GLA_KEY_WIDTH = GLA_HEADS * GLA_HEAD_K
GLA_WIDTH = GLA_HEADS * GLA_HEAD_V
GLA_GATE_RANK = 16
GLA_GATE_TAU = 16.0
DIFF_HEADS = 4
DIFF_HEAD_DIM = 128
DIFF_WIDTH = DIFF_HEADS * 2 * DIFF_HEAD_DIM
ROPE_THETA = 10000.0
NORM_EPS = 1e-6
SUBLN_EPS = 1e-5
LOG2E = math.log2(math.e)
DIFF_QSCALE = (DIFF_HEAD_DIM ** -0.5) * LOG2E

V7X_LANES = 128
V7X_VMEM_BYTES = 64 * 1024 * 1024
VMEM_LIMIT_BYTES = V7X_VMEM_BYTES - 8 * 1024 * 1024

COL_GQ = 0
COL_GK = COL_GQ + GLA_KEY_WIDTH
COL_GV = COL_GK + GLA_KEY_WIDTH
COL_GG = COL_GV + GLA_WIDTH
COL_DQ = COL_GG + GLA_WIDTH
COL_DK = COL_DQ + DIFF_WIDTH
COL_DV = COL_DK + DIFF_WIDTH
MAIN_COLS = COL_DV + DIFF_WIDTH

GLA_CHUNK = 256
GLA_LEVELS = int(math.log2(GLA_CHUNK))
GLA_FAST_MAX_DECAY = 64.0
Q_TILE = 512
ROW_CHUNK = 256
SCORE_BOUND_NO_SHIFT = 60.0


def _cparams(sem):
    return pltpu.CompilerParams(dimension_semantics=sem, vmem_limit_bytes=VMEM_LIMIT_BYTES)


def _log_decay(z):
    return (jnp.minimum(z, 0.0) - jnp.log1p(jnp.exp(-jnp.abs(z)))) * (1.0 / GLA_GATE_TAU)


def _log2_decay(z):
    soft = jnp.log2(1.0 + jnp.exp2(jnp.abs(z) * (-LOG2E)))
    return jnp.minimum(z, 0.0) * (LOG2E / GLA_GATE_TAU) - soft * (1.0 / GLA_GATE_TAU)


def _log_decay_split(z):
    la = _log_decay(z)
    la_hi = la.astype(BF16)
    la_lo = (la - la_hi.astype(F32)).astype(BF16)
    return jnp.concatenate([la_hi, la_lo], axis=1)


def _cumsum_rows(x):
    n = x.shape[0]
    row = lax.broadcasted_iota(jnp.int32, (n, 1), 0)
    s = 1
    while s < n:
        x = x + jnp.where(row >= s, pltpu.roll(x, s, 0), 0.0)
        s *= 2
    return x


def _in_proj_kernel(x_ref, nw_ref, wa_ref, wb_ref, wlr_ref, cos_ref, sin_ref,
                    qnw_ref, qnr_ref, knw_ref, knr_ref, w2_ref, gb_ref,
                    o_ref, olr_ref, b2_ref):
    tm = x_ref.shape[0]
    na = wa_ref.shape[1]
    dh = DIFF_HEAD_DIM
    n_half_heads = DIFF_WIDTH // dh
    for r0 in range(0, tm, ROW_CHUNK):
        rows = slice(r0, r0 + ROW_CHUNK)
        x = x_ref[rows, :]
        ms = jnp.mean(x * x, axis=-1, keepdims=True)
        n = (x * lax.rsqrt(ms + NORM_EPS) * nw_ref[...]).astype(BF16)
        lr = jnp.dot(n, wlr_ref[...], preferred_element_type=F32).astype(BF16)
        olr_ref[rows, :] = lr
        z = jnp.dot(lr, w2_ref[...], preferred_element_type=F32) + gb_ref[...]
        for h in range(GLA_HEADS):
            cols = slice(h * GLA_HEAD_K, (h + 1) * GLA_HEAD_K)
            b2_ref[rows, cols] = _cumsum_rows(_log2_decay(z[:, cols]))
        o_ref[rows, 0:na] = jnp.dot(n, wa_ref[...], preferred_element_type=F32).astype(BF16)
        ob = jnp.dot(n, wb_ref[...], preferred_element_type=F32)

        cosf = jnp.concatenate([cos_ref[rows, :], cos_ref[rows, :]], axis=1)
        sinf = jnp.concatenate([-sin_ref[rows, :], sin_ref[rows, :]], axis=1)
        tabs = ((cosf * (qnw_ref[...] * DIFF_QSCALE), sinf * (qnr_ref[...] * DIFF_QSCALE)),
                (cosf * knw_ref[...], sinf * knr_ref[...]))
        for i in range(2 * n_half_heads):
            ta, tb = tabs[i // n_half_heads]
            t = ob[:, i * dh:(i + 1) * dh]
            r = lax.rsqrt(jnp.mean(t * t, axis=-1, keepdims=True) + NORM_EPS)
            o_ref[rows, na + i * dh:na + (i + 1) * dh] = (
                r * (t * ta + pltpu.roll(t, dh // 2, 1) * tb)).astype(BF16)
        o_ref[rows, na + 2 * DIFF_WIDTH:] = ob[:, 2 * DIFF_WIDTH:].astype(BF16)


def _in_proj(x2, norm_w, w_a, w_b, w_lr, cos, sin, qnw, qnr, knw, knr, w2p, gate_b, tm):
    assert ROW_CHUNK == GLA_CHUNK
    m, d = x2.shape
    resident = pl.Buffered(1)
    half = DIFF_HEAD_DIM // 2
    vec = pl.BlockSpec((1, DIFF_HEAD_DIM), lambda i: (0, 0))
    return pl.pallas_call(
        _in_proj_kernel,
        grid=(m // tm,),
        in_specs=[
            pl.BlockSpec((tm, d), lambda i: (i, 0)),
            pl.BlockSpec((1, d), lambda i: (0, 0)),
            pl.BlockSpec((d, w_a.shape[1]), lambda i: (0, 0), pipeline_mode=resident),
            pl.BlockSpec((d, w_b.shape[1]), lambda i: (0, 0), pipeline_mode=resident),
            pl.BlockSpec((d, V7X_LANES), lambda i: (0, 0), pipeline_mode=resident),
            pl.BlockSpec((tm, half), lambda i: (i, 0)),
            pl.BlockSpec((tm, half), lambda i: (i, 0)),
            vec, vec, vec, vec,
            pl.BlockSpec((V7X_LANES, GLA_KEY_WIDTH), lambda i: (0, 0)),
            pl.BlockSpec((1, GLA_KEY_WIDTH), lambda i: (0, 0)),
        ],
        out_specs=[
            pl.BlockSpec((tm, MAIN_COLS), lambda i: (i, 0)),
            pl.BlockSpec((tm, V7X_LANES), lambda i: (i, 0)),
            pl.BlockSpec((tm, GLA_KEY_WIDTH), lambda i: (i, 0)),
        ],
        out_shape=[
            jax.ShapeDtypeStruct((m, MAIN_COLS), BF16),
            jax.ShapeDtypeStruct((m, V7X_LANES), BF16),
            jax.ShapeDtypeStruct((m, GLA_KEY_WIDTH), F32),
        ],
        compiler_params=_cparams(("parallel",)),
        name="in_proj",
    )(x2, norm_w, w_a, w_b, w_lr, cos, sin, qnw, qnr, knw, knr, w2p, gate_b)


def _gla_constants():
    c = GLA_CHUNK
    i = np.arange(c)[:, None]
    t = np.arange(c)[None, :]
    mats = [(t <= i), (t > i)]
    for lvl in range(GLA_LEVELS):
        s = 1 << lvl
        m = (i // (2 * s)) * (2 * s) + s - 1
        mats.append((t > np.minimum(i, m)) & (t <= np.maximum(i, m)))
    wstack = np.concatenate(mats, axis=0).astype(np.float32)
    x = i ^ t
    level = np.where(i > t, np.floor(np.log2(np.maximum(x, 1))), np.where(i == t, -1, -2))
    return wstack, level.astype(np.int32)


def _gla_kernel(q_ref, k_ref, v_ref, b2_ref, lr_ref, w2_ref, gb_ref, wstk_ref, lvl_ref, wo_ref,
                o_ref, wo_bf_ref, state_ref, *, seq):
    wo_bf_ref[...] = wo_ref[...].astype(BF16)
    c = GLA_CHUNK
    dk = GLA_HEAD_K
    n_chunks = seq // c
    w2 = w2_ref[...]
    gb = gb_ref[...]
    eye = (lax.broadcasted_iota(jnp.int32, (dk, dk), 0)
           == lax.broadcasted_iota(jnp.int32, (dk, dk), 1))
    nt = (((1,), (1,)), ((), ()))
    tn = (((0,), (0,)), ((), ()))

    def log_decay(r0):
        z = jnp.dot(lr_ref[pl.ds(r0, c), :], w2, preferred_element_type=F32) + gb
        return _log_decay_split(z)

    def finish(r0, o, kd, vb, e_last, state):
        o_ref[pl.ds(r0, c), :] = o.astype(o_ref.dtype)
        e_col = jnp.sum(jnp.where(eye, e_last, 0.0), axis=1, keepdims=True)
        state_ref[...] = state * e_col + lax.dot_general(kd, vb, tn, preferred_element_type=F32)

    tot = jnp.zeros((1, dk), F32)
    for ci in range(n_chunks):
        tot = jnp.maximum(tot, -b2_ref[(ci + 1) * c - 1:(ci + 1) * c, :])
    bounded = jnp.max(tot, axis=-1, keepdims=True)[0, 0] <= GLA_FAST_MAX_DECAY
    state_ref[...] = jnp.zeros_like(state_ref)

    def fast_chunk(ci, carry):
        r0 = pl.multiple_of(ci * c, c)
        qf = q_ref[pl.ds(r0, c), :].astype(F32)
        kf = k_ref[pl.ds(r0, c), :].astype(F32)
        vb = v_ref[pl.ds(r0, c), :]
        b2 = b2_ref[pl.ds(r0, c), :]
        eq = jnp.exp2(b2)
        qt = (qf * eq).astype(BF16)
        kt = kf * jnp.exp2(-b2)
        state = state_ref[...]
        causal = (lax.broadcasted_iota(jnp.int32, (c, c), 0)
                  >= lax.broadcasted_iota(jnp.int32, (c, c), 1))
        p = lax.dot_general(qt, kt.astype(BF16), nt, preferred_element_type=F32)
        o = jnp.dot(qt, state.astype(BF16), preferred_element_type=F32)
        o = o + jnp.dot(jnp.where(causal, p, 0.0).astype(BF16), vb, preferred_element_type=F32)
        e_last = eq[c - 1:c, :]
        finish(r0, o, (kt * e_last).astype(BF16), vb, e_last, state)
        return carry

    def level_chunk(ci, carry):
        r0 = pl.multiple_of(ci * c, c)
        lvl = lvl_ref[...]
        row = lax.broadcasted_iota(jnp.int32, (c, 1), 0)
        qf = q_ref[pl.ds(r0, c), :].astype(F32)
        kf = k_ref[pl.ds(r0, c), :].astype(F32)
        kb = k_ref[pl.ds(r0, c), :]
        vb = v_ref[pl.ds(r0, c), :]
        sums = jnp.dot(wstk_ref[...], log_decay(r0), preferred_element_type=F32)
        sums = sums[:, :dk] + sums[:, dk:]
        b = sums[0:c]
        b_rev = sums[c:2 * c]
        state = state_ref[...]
        o = jnp.dot((qf * jnp.exp(b)).astype(BF16), state.astype(BF16),
                    preferred_element_type=F32)
        attn = jnp.where(lvl == -1,
                         lax.dot_general(qf.astype(BF16), kb, nt, preferred_element_type=F32), 0.0)
        for l in range(GLA_LEVELS):
            e = jnp.exp(sums[(l + 2) * c:(l + 3) * c])
            upper = ((row >> l) & 1) == 1
            qt = jnp.where(upper, qf * e, 0.0).astype(BF16)
            kt = jnp.where(upper, 0.0, kf * e).astype(BF16)
            p = lax.dot_general(qt, kt, nt, preferred_element_type=F32)
            attn = jnp.where(lvl == l, p, attn)
        o = o + jnp.dot(attn.astype(BF16), vb, preferred_element_type=F32)
        finish(r0, o, (kf * jnp.exp(b_rev)).astype(BF16), vb, jnp.exp(b[c - 1:c, :]), state)
        return carry

    @pl.when(bounded)
    def _():
        lax.fori_loop(0, n_chunks, fast_chunk, 0, unroll=8)

    @pl.when(jnp.logical_not(bounded))
    def _():
        lax.fori_loop(0, n_chunks, level_chunk, 0)


def _slab_rows(n_rows, steps):
    rows = n_rows // steps
    assert rows * steps == n_rows and rows % 16 == 0
    return rows


def _gla(proj, b2, lr, w2p, gate_b, wstack, level, w_out, layer, batch, seq):
    m = proj.shape[0]
    kq = GLA_HEAD_K
    kv = GLA_HEAD_V
    rows = wstack.shape[0]
    wo_rows, wo_cols = w_out.shape[1:]
    slab = _slab_rows(wo_rows, batch * GLA_HEADS)
    return pl.pallas_call(
        functools.partial(_gla_kernel, seq=seq),
        grid=(batch, GLA_HEADS),
        in_specs=[
            pl.BlockSpec((seq, kq), lambda b, h: (b, COL_GQ // kq + h)),
            pl.BlockSpec((seq, kq), lambda b, h: (b, COL_GK // kq + h)),
            pl.BlockSpec((seq, kv), lambda b, h: (b, COL_GV // kv + h)),
            pl.BlockSpec((seq, kq), lambda b, h: (b, h)),
            pl.BlockSpec((seq, V7X_LANES), lambda b, h: (b, 0)),
            pl.BlockSpec((V7X_LANES, kq), lambda b, h: (0, h)),
            pl.BlockSpec((1, kq), lambda b, h: (0, h)),
            pl.BlockSpec((rows, GLA_CHUNK), lambda b, h: (0, 0)),
            pl.BlockSpec((GLA_CHUNK, GLA_CHUNK), lambda b, h: (0, 0)),
            pl.BlockSpec((None, slab, wo_cols), lambda b, h: (layer, b * GLA_HEADS + h, 0)),
        ],
        out_specs=[
            pl.BlockSpec((seq, kv), lambda b, h: (b, h)),
            pl.BlockSpec((slab, wo_cols), lambda b, h: (b * GLA_HEADS + h, 0)),
        ],
        out_shape=[
            jax.ShapeDtypeStruct((m, GLA_WIDTH), BF16),
            jax.ShapeDtypeStruct((wo_rows, wo_cols), BF16),
        ],
        scratch_shapes=[pltpu.VMEM((kq, kv), F32)],
        compiler_params=_cparams(("parallel", "parallel")),
        name="gla",
    )(proj, proj, proj, b2, lr, w2p, gate_b, wstack, level, w_out)


def _diff_kernel(q_ref, k_ref, v_ref, qnw_ref, knw_ref, lq1_ref, lk1_ref, lq2_ref, lk2_ref,
                 wu_ref, wd_ref, *rest, seq, lam_init, lr_cols):
    if lr_cols is None:
        o_ref, wu_bf_ref, wd_bf_ref = rest
    else:
        win_ref, o_ref, wu_bf_ref, wd_bf_ref, wa_bf_ref, wb_bf_ref, wlr_bf_ref = rest
        lr0, lr1 = lr_cols
        w = win_ref[...]
        wa_bf_ref[...] = w[:, :lr0].astype(BF16)
        wb_bf_ref[...] = w[:, lr1:].astype(BF16)
        pad = jnp.zeros((w.shape[0], V7X_LANES - (lr1 - lr0)), F32)
        wlr_bf_ref[...] = jnp.concatenate([w[:, lr0:lr1], pad], axis=1).astype(BF16)
    wu_bf_ref[...] = wu_ref[...].astype(BF16)
    wd_bf_ref[...] = wd_ref[...].astype(BF16)
    dh = DIFF_HEAD_DIM
    tq = Q_TILE
    lam = (jnp.exp(jnp.sum(lq1_ref[...] * lk1_ref[...], axis=-1, keepdims=True))
           - jnp.exp(jnp.sum(lq2_ref[...] * lk2_ref[...], axis=-1, keepdims=True))
           + lam_init)

    nt = (((1,), (1,)), ((), ()))
    tri = (lax.broadcasted_iota(jnp.int32, (tq, tq), 0)
           >= lax.broadcasted_iota(jnp.int32, (tq, tq), 1))

    def attend(subtract_max):
        for t in reversed(range(seq // tq)):
            r0 = t * tq
            ps, ls = [], []
            for comp in range(2):
                cols = slice(comp * dh, (comp + 1) * dh)
                q = q_ref[r0:r0 + tq, cols]
                s = lax.dot_general(q, k_ref[r0:r0 + tq, cols], nt, preferred_element_type=F32)
                s = jnp.where(tri, s, -jnp.inf)
                if t > 0:
                    s_off = lax.dot_general(q, k_ref[0:r0, cols], nt,
                                            preferred_element_type=F32)
                    s = jnp.concatenate([s_off, s], axis=1)
                if subtract_max:
                    s = s - jnp.max(s, axis=-1, keepdims=True)
                p = jnp.exp2(s)
                ls.append(jnp.sum(p, axis=-1, keepdims=True))
                ps.append(p.astype(BF16))
            ratio = (lam * ls[0] / ls[1]).astype(BF16)
            w = ps[0] - ratio * ps[1]
            o = jnp.dot(w, v_ref[0:r0 + tq, :], preferred_element_type=F32) * (1.0 / ls[0])
            o_ref[r0:r0 + tq, :] = o.astype(o_ref.dtype)

    bound = (dh * DIFF_QSCALE) * (jnp.max(jnp.abs(qnw_ref[...]), axis=-1, keepdims=True)
                             * jnp.max(jnp.abs(knw_ref[...]), axis=-1, keepdims=True))
    small = bound[0, 0] <= SCORE_BOUND_NO_SHIFT

    @pl.when(small)
    def _():
        attend(False)

    @pl.when(jnp.logical_not(small))
    def _():
        attend(True)


def _diff_attn(proj, qnw, knw, lq1, lk1, lq2, lk2, w_up, w_down, w_in, lr_cols, layer, batch, seq,
               lam_init):
    m = proj.shape[0]
    hw = 2 * DIFF_HEAD_DIM
    vec = pl.BlockSpec((1, DIFF_HEAD_DIM), lambda b, h: (0, 0))
    steps = batch * DIFF_HEADS
    up_rows, up_cols = w_up.shape[1:]
    dn_rows, dn_cols = w_down.shape[1:]
    up_slab = _slab_rows(up_rows, steps)
    dn_slab = _slab_rows(dn_rows, steps)
    step = lambda b, h: b * DIFF_HEADS + h
    has_next = layer + 1 < w_in.shape[0]
    in_rows, in_cols = w_in.shape[1:]
    in_slab = _slab_rows(in_rows, steps)
    lr0, lr1 = lr_cols
    next_in_specs, next_out_specs, next_out_shapes, next_args = [], [], [], []
    if has_next:
        next_in_specs = [pl.BlockSpec((None, in_slab, in_cols),
                                      lambda b, h: (layer + 1, step(b, h), 0))]
        next_args = [w_in]
        for cols in (lr0, in_cols - lr1, V7X_LANES):
            next_out_specs.append(pl.BlockSpec((in_slab, cols), lambda b, h: (step(b, h), 0)))
            next_out_shapes.append(jax.ShapeDtypeStruct((in_rows, cols), BF16))
    outs = pl.pallas_call(
        functools.partial(_diff_kernel, seq=seq, lam_init=lam_init,
                          lr_cols=lr_cols if has_next else None),
        grid=(batch, DIFF_HEADS),
        in_specs=[
            pl.BlockSpec((seq, hw), lambda b, h: (b, COL_DQ // hw + h)),
            pl.BlockSpec((seq, hw), lambda b, h: (b, COL_DK // hw + h)),
            pl.BlockSpec((seq, hw), lambda b, h: (b, COL_DV // hw + h)),
            vec, vec, vec, vec, vec, vec,
            pl.BlockSpec((None, up_slab, up_cols), lambda b, h: (layer, step(b, h), 0)),
            pl.BlockSpec((None, dn_slab, dn_cols), lambda b, h: (layer, step(b, h), 0)),
        ] + next_in_specs,
        out_specs=[
            pl.BlockSpec((seq, hw), lambda b, h: (b, h)),
            pl.BlockSpec((up_slab, up_cols), lambda b, h: (step(b, h), 0)),
            pl.BlockSpec((dn_slab, dn_cols), lambda b, h: (step(b, h), 0)),
        ] + next_out_specs,
        out_shape=[
            jax.ShapeDtypeStruct((m, DIFF_WIDTH), BF16),
            jax.ShapeDtypeStruct((up_rows, up_cols), BF16),
            jax.ShapeDtypeStruct((dn_rows, dn_cols), BF16),
        ] + next_out_shapes,
        compiler_params=_cparams(("parallel", "parallel")),
        name="diff_attn",
    )(proj, proj, proj, qnw, knw, lq1, lk1, lq2, lk2, w_up, w_down, *next_args)
    return outs[0], outs[1], outs[2], (tuple(outs[3:]) if has_next else None)


def _out_proj_kernel(x_ref, og_ref, g_ref, od_ref, onw_ref, sw_ref, wg_ref, wd_ref, o_ref,
                     *, lam_init):
    tm = x_ref.shape[0]
    hv = GLA_HEAD_V
    hd = 2 * DIFF_HEAD_DIM
    gla_eps = NORM_EPS * GLA_HEAD_K
    onw = onw_ref[...]
    sw = sw_ref[...] * (1.0 - lam_init)
    for r0 in range(0, tm, ROW_CHUNK):
        rows = slice(r0, r0 + ROW_CHUNK)
        ys = []
        for h in range(GLA_HEADS):
            o = og_ref[rows, h * hv:(h + 1) * hv].astype(F32)
            g = g_ref[rows, h * hv:(h + 1) * hv].astype(F32)
            y = o * lax.rsqrt(jnp.mean(o * o, axis=-1, keepdims=True) + gla_eps) * onw
            ys.append((y * (g * jax.nn.sigmoid(g))).astype(BF16))
        acc = jnp.dot(jnp.concatenate(ys, axis=1), wg_ref[...], preferred_element_type=F32)
        ys = []
        for h in range(DIFF_HEADS):
            o = od_ref[rows, h * hd:(h + 1) * hd].astype(F32)
            y = o * lax.rsqrt(jnp.mean(o * o, axis=-1, keepdims=True) + SUBLN_EPS) * sw
            ys.append(y.astype(BF16))
        acc = acc + jnp.dot(jnp.concatenate(ys, axis=1), wd_ref[...],
                            preferred_element_type=F32)
        o_ref[rows, :] = x_ref[rows, :] + acc


def _out_proj(x2, og, proj, od, out_norm_w, subln_w, w_out, tm, lam_init):
    m, d = x2.shape
    return pl.pallas_call(
        functools.partial(_out_proj_kernel, lam_init=lam_init),
        grid=(m // tm,),
        in_specs=[
            pl.BlockSpec((tm, d), lambda i: (i, 0)),
            pl.BlockSpec((tm, GLA_WIDTH), lambda i: (i, 0)),
            pl.BlockSpec((tm, GLA_WIDTH), lambda i: (i, COL_GG // GLA_WIDTH)),
            pl.BlockSpec((tm, DIFF_WIDTH), lambda i: (i, 0)),
            pl.BlockSpec((1, GLA_HEAD_V), lambda i: (0, 0)),
            pl.BlockSpec((1, 2 * DIFF_HEAD_DIM), lambda i: (0, 0)),
            pl.BlockSpec((GLA_WIDTH, d), lambda i: (0, 0)),
            pl.BlockSpec((DIFF_WIDTH, d), lambda i: (GLA_WIDTH // DIFF_WIDTH, 0)),
        ],
        out_specs=pl.BlockSpec((tm, d), lambda i: (i, 0)),
        out_shape=jax.ShapeDtypeStruct((m, d), F32),
        compiler_params=_cparams(("parallel",)),
        name="out_proj",
    )(x2, og, proj, od, out_norm_w, subln_w, w_out, w_out)


def _mlp_kernel(x_ref, nw_ref, wu_ref, wd_ref, o_ref, n_ref):
    j = pl.program_id(1)
    tm = x_ref.shape[0]

    def ff(n):
        h = jnp.dot(n, wu_ref[...], preferred_element_type=F32)
        h = jnp.square(jnp.maximum(h, 0.0)).astype(BF16)
        return jnp.dot(h, wd_ref[...], preferred_element_type=F32)

    @pl.when(j == 0)
    def _():
        for r0 in range(0, tm, ROW_CHUNK):
            x = x_ref[r0:r0 + ROW_CHUNK, :]
            ms = jnp.mean(x * x, axis=-1, keepdims=True)
            n = (x * lax.rsqrt(ms + NORM_EPS) * nw_ref[...]).astype(BF16)
            n_ref[r0:r0 + ROW_CHUNK, :] = n
            o_ref[r0:r0 + ROW_CHUNK, :] = x + ff(n)

    @pl.when(j > 0)
    def _():
        for r0 in range(0, tm, ROW_CHUNK):
            o_ref[r0:r0 + ROW_CHUNK, :] += ff(n_ref[r0:r0 + ROW_CHUNK, :])


def _mlp(x2, norm_w, w_up, w_down, tm, tf):
    m, d = x2.shape
    f = w_up.shape[1]
    return pl.pallas_call(
        _mlp_kernel,
        grid=(m // tm, f // tf),
        in_specs=[
            pl.BlockSpec((tm, d), lambda i, j: (i, 0)),
            pl.BlockSpec((1, d), lambda i, j: (0, 0)),
            pl.BlockSpec((d, tf), lambda i, j: (0, j)),
            pl.BlockSpec((tf, d), lambda i, j: (j, 0)),
        ],
        out_specs=pl.BlockSpec((tm, d), lambda i, j: (i, 0)),
        out_shape=jax.ShapeDtypeStruct((m, d), F32),
        scratch_shapes=[pltpu.VMEM((tm, d), BF16)],
        compiler_params=_cparams(("parallel", "arbitrary")),
        name="mlp",
    )(x2, norm_w, w_up, w_down)


def _lambda_init(layer_idx):
    return 0.8 - 0.6 * math.exp(-0.3 * layer_idx)


def kernel(x, positions, attn_norm_w, w_in, gla_gate_w2, gla_gate_b, gla_out_norm_w,
           diff_q_norm_w, diff_k_norm_w, diff_lambda_q1, diff_lambda_k1, diff_lambda_q2,
           diff_lambda_k2, diff_subln_w, w_out, mlp_norm_w, w_up, w_down):
    batch, seq, d = x.shape
    depth = w_in.shape[0]
    m = batch * seq
    assert d == GLA_WIDTH + DIFF_WIDTH and seq % GLA_CHUNK == 0 and seq % Q_TILE == 0
    tm_proj = min(512, m)
    tm_out = min(512, m)
    tm_mlp = min(512, m)

    half = DIFF_HEAD_DIM // 2
    inv_freq = ROPE_THETA ** (-jnp.arange(0, DIFF_HEAD_DIM, 2, dtype=F32) / DIFF_HEAD_DIM)
    ang = (positions.astype(F32)[..., None] * inv_freq).reshape(m, half)
    cos, sin = jnp.cos(ang), jnp.sin(ang)

    wstack_np, level_np = _gla_constants()
    wstack = jnp.asarray(wstack_np, dtype=BF16)
    level = jnp.asarray(level_np)

    sizes = (GLA_KEY_WIDTH, GLA_KEY_WIDTH, GLA_WIDTH, GLA_WIDTH, GLA_GATE_RANK,
             DIFF_WIDTH, DIFF_WIDTH, DIFF_WIDTH)
    offs = np.concatenate([[0], np.cumsum(sizes)])
    lr0, lr1 = int(offs[4]), int(offs[5])

    w_in_parts = (w_in[0, :, :lr0].astype(BF16), w_in[0, :, lr1:].astype(BF16),
                  jnp.pad(w_in[0, :, lr0:lr1],
                          ((0, 0), (0, V7X_LANES - GLA_GATE_RANK))).astype(BF16))

    x2 = x.reshape(m, d)
    for layer in range(depth):
        w2p = jnp.pad(gla_gate_w2[layer], ((0, V7X_LANES - GLA_GATE_RANK), (0, 0))).astype(BF16)
        gate_b = gla_gate_b[layer][None, :]

        qnw = diff_q_norm_w[layer][None, :]
        knw = diff_k_norm_w[layer][None, :]
        proj, lr, b2 = _in_proj(x2, attn_norm_w[layer][None, :], *w_in_parts, cos, sin,
                                qnw, jnp.roll(qnw, half, axis=1), knw, jnp.roll(knw, half, axis=1),
                                w2p, gate_b, tm_proj)
        o_gla, w_out_b = _gla(proj, b2, lr, w2p, gate_b, wstack, level, w_out, layer, batch, seq)
        o_diff, w_up_b, w_down_b, w_in_parts = _diff_attn(
            proj, qnw, knw,
            diff_lambda_q1[layer][None, :], diff_lambda_k1[layer][None, :],
            diff_lambda_q2[layer][None, :], diff_lambda_k2[layer][None, :],
            w_up, w_down, w_in, (lr0, lr1), layer, batch, seq, _lambda_init(layer))
        x2 = _out_proj(x2, o_gla, proj, o_diff, gla_out_norm_w[layer][None, :],
                       diff_subln_w[layer][None, :], w_out_b, tm_out, _lambda_init(layer))
        x2 = _mlp(x2, mlp_norm_w[layer][None, :], w_up_b, w_down_b, tm_mlp, 2048)
    return x2.reshape(batch, seq, d)
```

```python
import functools
import math

import numpy as np
import jax
import jax.numpy as jnp
from jax import lax
from jax.experimental import pallas as pl
from jax.experimental.pallas import tpu as pltpu

F32 = jnp.float32
BF16 = jnp.bfloat16

GLA_HEADS = 4
GLA_HEAD_K = 128
GLA_HEAD_V = 256
GLA_KEY_WIDTH = GLA_HEADS * GLA_HEAD_K
GLA_WIDTH = GLA_HEADS * GLA_HEAD_V
GLA_GATE_RANK = 16
GLA_GATE_TAU = 16.0
DIFF_HEADS = 4
DIFF_HEAD_DIM = 128
DIFF_WIDTH = DIFF_HEADS * 2 * DIFF_HEAD_DIM
ROPE_THETA = 10000.0
NORM_EPS = 1e-6
SUBLN_EPS = 1e-5
LOG2E = math.log2(math.e)
DIFF_QSCALE = (DIFF_HEAD_DIM ** -0.5) * LOG2E

V7X_LANES = 128
V7X_VMEM_BYTES = 64 * 1024 * 1024
VMEM_LIMIT_BYTES = V7X_VMEM_BYTES - 8 * 1024 * 1024

COL_GQ = 0
COL_GK = COL_GQ + GLA_KEY_WIDTH
COL_GV = COL_GK + GLA_KEY_WIDTH
COL_GG = COL_GV + GLA_WIDTH
COL_DQ = COL_GG + GLA_WIDTH
COL_DK = COL_DQ + DIFF_WIDTH
COL_DV = COL_DK + DIFF_WIDTH
MAIN_COLS = COL_DV + DIFF_WIDTH

GLA_CHUNK = 256
GLA_LEVELS = int(math.log2(GLA_CHUNK))
GLA_FAST_MAX_DECAY = 64.0
Q_TILE = 512
ROW_CHUNK = 256
W_IN_CAST_ROWS = 128
SCORE_BOUND_NO_SHIFT = 60.0


def _cparams(sem):
    return pltpu.CompilerParams(dimension_semantics=sem, vmem_limit_bytes=VMEM_LIMIT_BYTES)


def _log_decay(z):
    return (jnp.minimum(z, 0.0) - jnp.log1p(jnp.exp(-jnp.abs(z)))) * (1.0 / GLA_GATE_TAU)


def _log2_decay(z):
    soft = jnp.log2(1.0 + jnp.exp2(jnp.abs(z) * (-LOG2E)))
    return jnp.minimum(z, 0.0) * (LOG2E / GLA_GATE_TAU) - soft * (1.0 / GLA_GATE_TAU)


def _log_decay_split(z):
    la = _log_decay(z)
    la_hi = la.astype(BF16)
    la_lo = (la - la_hi.astype(F32)).astype(BF16)
    return jnp.concatenate([la_hi, la_lo], axis=1)


def _cumsum_rows(x):
    n = x.shape[0]
    row = lax.broadcasted_iota(jnp.int32, (n, 1), 0)
    s = 1
    while s < n:
        x = x + jnp.where(row >= s, pltpu.roll(x, s, 0), 0.0)
        s *= 2
    return x


def _w_in_parts_kernel(win_ref, wa_ref, wb_ref, wlr_ref, *, lr_cols):
    lr0, lr1 = lr_cols
    w = win_ref[...]
    wa_ref[...] = w[:, :lr0].astype(BF16)
    wb_ref[...] = w[:, lr1:].astype(BF16)
    pad = jnp.zeros((w.shape[0], V7X_LANES - (lr1 - lr0)), F32)
    wlr_ref[...] = jnp.concatenate([w[:, lr0:lr1], pad], axis=1).astype(BF16)


def _w_in_parts(w_in, layer, lr_cols):
    rows, cols = w_in.shape[1:]
    lr0, lr1 = lr_cols
    slab = min(W_IN_CAST_ROWS, rows)
    widths = (lr0, cols - lr1, V7X_LANES)
    return pl.pallas_call(
        functools.partial(_w_in_parts_kernel, lr_cols=lr_cols),
        grid=(rows // slab,),
        in_specs=[pl.BlockSpec((None, slab, cols), lambda i: (layer, i, 0))],
        out_specs=[pl.BlockSpec((slab, wd), lambda i: (i, 0)) for wd in widths],
        out_shape=[jax.ShapeDtypeStruct((rows, wd), BF16) for wd in widths],
        compiler_params=_cparams(("parallel",)),
        name="w_in_cast",
    )(w_in)


def _in_proj_kernel(x_ref, nw_ref, wa_ref, wb_ref, wlr_ref, cos_ref, sin_ref,
                    qnw_ref, qnr_ref, knw_ref, knr_ref, w2_ref, gb_ref,
                    o_ref, olr_ref, b2_ref):
    tm = x_ref.shape[0]
    na = wa_ref.shape[1]
    dh = DIFF_HEAD_DIM
    n_half_heads = DIFF_WIDTH // dh
    for r0 in range(0, tm, ROW_CHUNK):
        rows = slice(r0, r0 + ROW_CHUNK)
        x = x_ref[rows, :]
        ms = jnp.mean(x * x, axis=-1, keepdims=True)
        n = (x * lax.rsqrt(ms + NORM_EPS) * nw_ref[...]).astype(BF16)
        lr = jnp.dot(n, wlr_ref[...], preferred_element_type=F32).astype(BF16)
        olr_ref[rows, :] = lr
        z = jnp.dot(lr, w2_ref[...], preferred_element_type=F32) + gb_ref[...]
        for h in range(GLA_HEADS):
            cols = slice(h * GLA_HEAD_K, (h + 1) * GLA_HEAD_K)
            b2_ref[rows, cols] = _cumsum_rows(_log2_decay(z[:, cols]))
        o_ref[rows, 0:na] = jnp.dot(n, wa_ref[...], preferred_element_type=F32).astype(BF16)
        ob = jnp.dot(n, wb_ref[...], preferred_element_type=F32)

        cosf = jnp.concatenate([cos_ref[rows, :], cos_ref[rows, :]], axis=1)
        sinf = jnp.concatenate([-sin_ref[rows, :], sin_ref[rows, :]], axis=1)
        tabs = ((cosf * (qnw_ref[...] * DIFF_QSCALE), sinf * (qnr_ref[...] * DIFF_QSCALE)),
                (cosf * knw_ref[...], sinf * knr_ref[...]))
        for i in range(2 * n_half_heads):
            ta, tb = tabs[i // n_half_heads]
            t = ob[:, i * dh:(i + 1) * dh]
            r = lax.rsqrt(jnp.mean(t * t, axis=-1, keepdims=True) + NORM_EPS)
            o_ref[rows, na + i * dh:na + (i + 1) * dh] = (
                r * (t * ta + pltpu.roll(t, dh // 2, 1) * tb)).astype(BF16)
        o_ref[rows, na + 2 * DIFF_WIDTH:] = ob[:, 2 * DIFF_WIDTH:].astype(BF16)


def _in_proj(x2, norm_w, w_a, w_b, w_lr, cos, sin, qnw, qnr, knw, knr, w2p, gate_b, tm):
    assert ROW_CHUNK == GLA_CHUNK
    m, d = x2.shape
    resident = pl.Buffered(1)
    half = DIFF_HEAD_DIM // 2
    vec = pl.BlockSpec((1, DIFF_HEAD_DIM), lambda i: (0, 0))
    return pl.pallas_call(
        _in_proj_kernel,
        grid=(m // tm,),
        in_specs=[
            pl.BlockSpec((tm, d), lambda i: (i, 0)),
            pl.BlockSpec((1, d), lambda i: (0, 0)),
            pl.BlockSpec((d, w_a.shape[1]), lambda i: (0, 0), pipeline_mode=resident),
            pl.BlockSpec((d, w_b.shape[1]), lambda i: (0, 0), pipeline_mode=resident),
            pl.BlockSpec((d, V7X_LANES), lambda i: (0, 0), pipeline_mode=resident),
            pl.BlockSpec((tm, half), lambda i: (i, 0)),
            pl.BlockSpec((tm, half), lambda i: (i, 0)),
            vec, vec, vec, vec,
            pl.BlockSpec((V7X_LANES, GLA_KEY_WIDTH), lambda i: (0, 0)),
            pl.BlockSpec((1, GLA_KEY_WIDTH), lambda i: (0, 0)),
        ],
        out_specs=[
            pl.BlockSpec((tm, MAIN_COLS), lambda i: (i, 0)),
            pl.BlockSpec((tm, V7X_LANES), lambda i: (i, 0)),
            pl.BlockSpec((tm, GLA_KEY_WIDTH), lambda i: (i, 0)),
        ],
        out_shape=[
            jax.ShapeDtypeStruct((m, MAIN_COLS), BF16),
            jax.ShapeDtypeStruct((m, V7X_LANES), BF16),
            jax.ShapeDtypeStruct((m, GLA_KEY_WIDTH), F32),
        ],
        compiler_params=_cparams(("parallel",)),
        name="in_proj",
    )(x2, norm_w, w_a, w_b, w_lr, cos, sin, qnw, qnr, knw, knr, w2p, gate_b)


def _gla_constants():
    c = GLA_CHUNK
    i = np.arange(c)[:, None]
    t = np.arange(c)[None, :]
    mats = [(t <= i), (t > i)]
    for lvl in range(GLA_LEVELS):
        s = 1 << lvl
        m = (i // (2 * s)) * (2 * s) + s - 1
        mats.append((t > np.minimum(i, m)) & (t <= np.maximum(i, m)))
    wstack = np.concatenate(mats, axis=0).astype(np.float32)
    x = i ^ t
    level = np.where(i > t, np.floor(np.log2(np.maximum(x, 1))), np.where(i == t, -1, -2))
    return wstack, level.astype(np.int32)


def _gla_kernel(q_ref, k_ref, v_ref, b2_ref, lr_ref, w2_ref, gb_ref, wstk_ref, lvl_ref, wo_ref,
                o_ref, wo_bf_ref, state_ref, *, seq):
    wo_bf_ref[...] = wo_ref[...].astype(BF16)
    c = GLA_CHUNK
    dk = GLA_HEAD_K
    n_chunks = seq // c
    w2 = w2_ref[...]
    gb = gb_ref[...]
    eye = (lax.broadcasted_iota(jnp.int32, (dk, dk), 0)
           == lax.broadcasted_iota(jnp.int32, (dk, dk), 1))
    nt = (((1,), (1,)), ((), ()))
    tn = (((0,), (0,)), ((), ()))

    def log_decay(r0):
        z = jnp.dot(lr_ref[pl.ds(r0, c), :], w2, preferred_element_type=F32) + gb
        return _log_decay_split(z)

    def finish(r0, o, kd, vb, e_last, state):
        o_ref[pl.ds(r0, c), :] = o.astype(o_ref.dtype)
        e_col = jnp.sum(jnp.where(eye, e_last, 0.0), axis=1, keepdims=True)
        state_ref[...] = state * e_col + lax.dot_general(kd, vb, tn, preferred_element_type=F32)

    tot = jnp.zeros((1, dk), F32)
    for ci in range(n_chunks):
        tot = jnp.maximum(tot, -b2_ref[(ci + 1) * c - 1:(ci + 1) * c, :])
    bounded = jnp.max(tot, axis=-1, keepdims=True)[0, 0] <= GLA_FAST_MAX_DECAY
    state_ref[...] = jnp.zeros_like(state_ref)

    def fast_chunk(ci, carry):
        r0 = pl.multiple_of(ci * c, c)
        qf = q_ref[pl.ds(r0, c), :].astype(F32)
        kf = k_ref[pl.ds(r0, c), :].astype(F32)
        vb = v_ref[pl.ds(r0, c), :]
        b2 = b2_ref[pl.ds(r0, c), :]
        eq = jnp.exp2(b2)
        qt = (qf * eq).astype(BF16)
        kt = kf * jnp.exp2(-b2)
        state = state_ref[...]
        causal = (lax.broadcasted_iota(jnp.int32, (c, c), 0)
                  >= lax.broadcasted_iota(jnp.int32, (c, c), 1))
        p = lax.dot_general(qt, kt.astype(BF16), nt, preferred_element_type=F32)
        o = jnp.dot(qt, state.astype(BF16), preferred_element_type=F32)
        o = o + jnp.dot(jnp.where(causal, p, 0.0).astype(BF16), vb, preferred_element_type=F32)
        e_last = eq[c - 1:c, :]
        finish(r0, o, (kt * e_last).astype(BF16), vb, e_last, state)
        return carry

    def level_chunk(ci, carry):
        r0 = pl.multiple_of(ci * c, c)
        lvl = lvl_ref[...]
        row = lax.broadcasted_iota(jnp.int32, (c, 1), 0)
        qf = q_ref[pl.ds(r0, c), :].astype(F32)
        kf = k_ref[pl.ds(r0, c), :].astype(F32)
        kb = k_ref[pl.ds(r0, c), :]
        vb = v_ref[pl.ds(r0, c), :]
        sums = jnp.dot(wstk_ref[...], log_decay(r0), preferred_element_type=F32)
        sums = sums[:, :dk] + sums[:, dk:]
        b = sums[0:c]
        b_rev = sums[c:2 * c]
        state = state_ref[...]
        o = jnp.dot((qf * jnp.exp(b)).astype(BF16), state.astype(BF16),
                    preferred_element_type=F32)
        attn = jnp.where(lvl == -1,
                         lax.dot_general(qf.astype(BF16), kb, nt, preferred_element_type=F32), 0.0)
        for l in range(GLA_LEVELS):
            e = jnp.exp(sums[(l + 2) * c:(l + 3) * c])
            upper = ((row >> l) & 1) == 1
            qt = jnp.where(upper, qf * e, 0.0).astype(BF16)
            kt = jnp.where(upper, 0.0, kf * e).astype(BF16)
            p = lax.dot_general(qt, kt, nt, preferred_element_type=F32)
            attn = jnp.where(lvl == l, p, attn)
        o = o + jnp.dot(attn.astype(BF16), vb, preferred_element_type=F32)
        finish(r0, o, (kf * jnp.exp(b_rev)).astype(BF16), vb, jnp.exp(b[c - 1:c, :]), state)
        return carry

    @pl.when(bounded)
    def _():
        lax.fori_loop(0, n_chunks, fast_chunk, 0, unroll=8)

    @pl.when(jnp.logical_not(bounded))
    def _():
        lax.fori_loop(0, n_chunks, level_chunk, 0)


def _slab_rows(n_rows, steps):
    rows = n_rows // steps
    assert rows * steps == n_rows and rows % 16 == 0
    return rows


def _gla(proj, b2, lr, w2p, gate_b, wstack, level, w_out, layer, batch, seq):
    m = proj.shape[0]
    kq = GLA_HEAD_K
    kv = GLA_HEAD_V
    rows = wstack.shape[0]
    wo_rows, wo_cols = w_out.shape[1:]
    slab = _slab_rows(wo_rows, batch * GLA_HEADS)
    return pl.pallas_call(
        functools.partial(_gla_kernel, seq=seq),
        grid=(batch, GLA_HEADS),
        in_specs=[
            pl.BlockSpec((seq, kq), lambda b, h: (b, COL_GQ // kq + h)),
            pl.BlockSpec((seq, kq), lambda b, h: (b, COL_GK // kq + h)),
            pl.BlockSpec((seq, kv), lambda b, h: (b, COL_GV // kv + h)),
            pl.BlockSpec((seq, kq), lambda b, h: (b, h)),
            pl.BlockSpec((seq, V7X_LANES), lambda b, h: (b, 0)),
            pl.BlockSpec((V7X_LANES, kq), lambda b, h: (0, h)),
            pl.BlockSpec((1, kq), lambda b, h: (0, h)),
            pl.BlockSpec((rows, GLA_CHUNK), lambda b, h: (0, 0)),
            pl.BlockSpec((GLA_CHUNK, GLA_CHUNK), lambda b, h: (0, 0)),
            pl.BlockSpec((None, slab, wo_cols), lambda b, h: (layer, b * GLA_HEADS + h, 0)),
        ],
        out_specs=[
            pl.BlockSpec((seq, kv), lambda b, h: (b, h)),
            pl.BlockSpec((slab, wo_cols), lambda b, h: (b * GLA_HEADS + h, 0)),
        ],
        out_shape=[
            jax.ShapeDtypeStruct((m, GLA_WIDTH), BF16),
            jax.ShapeDtypeStruct((wo_rows, wo_cols), BF16),
        ],
        scratch_shapes=[pltpu.VMEM((kq, kv), F32)],
        compiler_params=_cparams(("parallel", "parallel")),
        name="gla",
    )(proj, proj, proj, b2, lr, w2p, gate_b, wstack, level, w_out)


def _diff_kernel(q_ref, k_ref, v_ref, qnw_ref, knw_ref, lq1_ref, lk1_ref, lq2_ref, lk2_ref,
                 wu_ref, wd_ref, *rest, seq, lam_init, lr_cols):
    if lr_cols is None:
        o_ref, wu_bf_ref, wd_bf_ref = rest
    else:
        win_ref, o_ref, wu_bf_ref, wd_bf_ref, wa_bf_ref, wb_bf_ref, wlr_bf_ref = rest
        _w_in_parts_kernel(win_ref, wa_bf_ref, wb_bf_ref, wlr_bf_ref, lr_cols=lr_cols)
    wu_bf_ref[...] = wu_ref[...].astype(BF16)
    wd_bf_ref[...] = wd_ref[...].astype(BF16)
    dh = DIFF_HEAD_DIM
    tq = Q_TILE
    lam = (jnp.exp(jnp.sum(lq1_ref[...] * lk1_ref[...], axis=-1, keepdims=True))
           - jnp.exp(jnp.sum(lq2_ref[...] * lk2_ref[...], axis=-1, keepdims=True))
           + lam_init)

    nt = (((1,), (1,)), ((), ()))
    tri = (lax.broadcasted_iota(jnp.int32, (tq, tq), 0)
           >= lax.broadcasted_iota(jnp.int32, (tq, tq), 1))

    def attend(subtract_max):
        for t in reversed(range(seq // tq)):
            r0 = t * tq
            ps, ls = [], []
            for comp in range(2):
                cols = slice(comp * dh, (comp + 1) * dh)
                q = q_ref[r0:r0 + tq, cols]
                s = lax.dot_general(q, k_ref[r0:r0 + tq, cols], nt, preferred_element_type=F32)
                s = jnp.where(tri, s, -jnp.inf)
                if t > 0:
                    s_off = lax.dot_general(q, k_ref[0:r0, cols], nt,
                                            preferred_element_type=F32)
                    s = jnp.concatenate([s_off, s], axis=1)
                if subtract_max:
                    s = s - jnp.max(s, axis=-1, keepdims=True)
                p = jnp.exp2(s)
                ls.append(jnp.sum(p, axis=-1, keepdims=True))
                ps.append(p.astype(BF16))
            ratio = (lam * ls[0] / ls[1]).astype(BF16)
            w = ps[0] - ratio * ps[1]
            o = jnp.dot(w, v_ref[0:r0 + tq, :], preferred_element_type=F32) * (1.0 / ls[0])
            o_ref[r0:r0 + tq, :] = o.astype(o_ref.dtype)

    bound = (dh * DIFF_QSCALE) * (jnp.max(jnp.abs(qnw_ref[...]), axis=-1, keepdims=True)
                             * jnp.max(jnp.abs(knw_ref[...]), axis=-1, keepdims=True))
    small = bound[0, 0] <= SCORE_BOUND_NO_SHIFT

    @pl.when(small)
    def _():
        attend(False)

    @pl.when(jnp.logical_not(small))
    def _():
        attend(True)


def _diff_attn(proj, qnw, knw, lq1, lk1, lq2, lk2, w_up, w_down, w_in, lr_cols, layer, batch, seq,
               lam_init):
    m = proj.shape[0]
    hw = 2 * DIFF_HEAD_DIM
    vec = pl.BlockSpec((1, DIFF_HEAD_DIM), lambda b, h: (0, 0))
    steps = batch * DIFF_HEADS
    up_rows, up_cols = w_up.shape[1:]
    dn_rows, dn_cols = w_down.shape[1:]
    up_slab = _slab_rows(up_rows, steps)
    dn_slab = _slab_rows(dn_rows, steps)
    step = lambda b, h: b * DIFF_HEADS + h
    has_next = layer + 1 < w_in.shape[0]
    in_rows, in_cols = w_in.shape[1:]
    in_slab = _slab_rows(in_rows, steps)
    lr0, lr1 = lr_cols
    next_in_specs, next_out_specs, next_out_shapes, next_args = [], [], [], []
    if has_next:
        next_in_specs = [pl.BlockSpec((None, in_slab, in_cols),
                                      lambda b, h: (layer + 1, step(b, h), 0))]
        next_args = [w_in]
        for cols in (lr0, in_cols - lr1, V7X_LANES):
            next_out_specs.append(pl.BlockSpec((in_slab, cols), lambda b, h: (step(b, h), 0)))
            next_out_shapes.append(jax.ShapeDtypeStruct((in_rows, cols), BF16))
    outs = pl.pallas_call(
        functools.partial(_diff_kernel, seq=seq, lam_init=lam_init,
                          lr_cols=lr_cols if has_next else None),
        grid=(batch, DIFF_HEADS),
        in_specs=[
            pl.BlockSpec((seq, hw), lambda b, h: (b, COL_DQ // hw + h)),
            pl.BlockSpec((seq, hw), lambda b, h: (b, COL_DK // hw + h)),
            pl.BlockSpec((seq, hw), lambda b, h: (b, COL_DV // hw + h)),
            vec, vec, vec, vec, vec, vec,
            pl.BlockSpec((None, up_slab, up_cols), lambda b, h: (layer, step(b, h), 0)),
            pl.BlockSpec((None, dn_slab, dn_cols), lambda b, h: (layer, step(b, h), 0)),
        ] + next_in_specs,
        out_specs=[
            pl.BlockSpec((seq, hw), lambda b, h: (b, h)),
            pl.BlockSpec((up_slab, up_cols), lambda b, h: (step(b, h), 0)),
            pl.BlockSpec((dn_slab, dn_cols), lambda b, h: (step(b, h), 0)),
        ] + next_out_specs,
        out_shape=[
            jax.ShapeDtypeStruct((m, DIFF_WIDTH), BF16),
            jax.ShapeDtypeStruct((up_rows, up_cols), BF16),
            jax.ShapeDtypeStruct((dn_rows, dn_cols), BF16),
        ] + next_out_shapes,
        compiler_params=_cparams(("parallel", "parallel")),
        name="diff_attn",
    )(proj, proj, proj, qnw, knw, lq1, lk1, lq2, lk2, w_up, w_down, *next_args)
    return outs[0], outs[1], outs[2], (tuple(outs[3:]) if has_next else None)


def _out_proj_kernel(x_ref, og_ref, g_ref, od_ref, onw_ref, sw_ref, wg_ref, wd_ref, o_ref,
                     *, lam_init):
    tm = x_ref.shape[0]
    hv = GLA_HEAD_V
    hd = 2 * DIFF_HEAD_DIM
    gla_eps = NORM_EPS * GLA_HEAD_K
    onw = onw_ref[...]
    sw = sw_ref[...] * (1.0 - lam_init)
    for r0 in range(0, tm, ROW_CHUNK):
        rows = slice(r0, r0 + ROW_CHUNK)
        ys = []
        for h in range(GLA_HEADS):
            o = og_ref[rows, h * hv:(h + 1) * hv].astype(F32)
            g = g_ref[rows, h * hv:(h + 1) * hv].astype(F32)
            y = o * lax.rsqrt(jnp.mean(o * o, axis=-1, keepdims=True) + gla_eps) * onw
            ys.append((y * (g * jax.nn.sigmoid(g))).astype(BF16))
        acc = jnp.dot(jnp.concatenate(ys, axis=1), wg_ref[...], preferred_element_type=F32)
        ys = []
        for h in range(DIFF_HEADS):
            o = od_ref[rows, h * hd:(h + 1) * hd].astype(F32)
            y = o * lax.rsqrt(jnp.mean(o * o, axis=-1, keepdims=True) + SUBLN_EPS) * sw
            ys.append(y.astype(BF16))
        acc = acc + jnp.dot(jnp.concatenate(ys, axis=1), wd_ref[...],
                            preferred_element_type=F32)
        o_ref[rows, :] = x_ref[rows, :] + acc


def _out_proj(x2, og, proj, od, out_norm_w, subln_w, w_out, tm, lam_init):
    m, d = x2.shape
    return pl.pallas_call(
        functools.partial(_out_proj_kernel, lam_init=lam_init),
        grid=(m // tm,),
        in_specs=[
            pl.BlockSpec((tm, d), lambda i: (i, 0)),
            pl.BlockSpec((tm, GLA_WIDTH), lambda i: (i, 0)),
            pl.BlockSpec((tm, GLA_WIDTH), lambda i: (i, COL_GG // GLA_WIDTH)),
            pl.BlockSpec((tm, DIFF_WIDTH), lambda i: (i, 0)),
            pl.BlockSpec((1, GLA_HEAD_V), lambda i: (0, 0)),
            pl.BlockSpec((1, 2 * DIFF_HEAD_DIM), lambda i: (0, 0)),
            pl.BlockSpec((GLA_WIDTH, d), lambda i: (0, 0)),
            pl.BlockSpec((DIFF_WIDTH, d), lambda i: (GLA_WIDTH // DIFF_WIDTH, 0)),
        ],
        out_specs=pl.BlockSpec((tm, d), lambda i: (i, 0)),
        out_shape=jax.ShapeDtypeStruct((m, d), F32),
        compiler_params=_cparams(("parallel",)),
        name="out_proj",
    )(x2, og, proj, od, out_norm_w, subln_w, w_out, w_out)


def _mlp_kernel(x_ref, nw_ref, wu_ref, wd_ref, o_ref, n_ref):
    j = pl.program_id(1)
    tm = x_ref.shape[0]

    def ff(n):
        h = jnp.dot(n, wu_ref[...], preferred_element_type=F32)
        h = jnp.square(jnp.maximum(h, 0.0)).astype(BF16)
        return jnp.dot(h, wd_ref[...], preferred_element_type=F32)

    @pl.when(j == 0)
    def _():
        for r0 in range(0, tm, ROW_CHUNK):
            x = x_ref[r0:r0 + ROW_CHUNK, :]
            ms = jnp.mean(x * x, axis=-1, keepdims=True)
            n = (x * lax.rsqrt(ms + NORM_EPS) * nw_ref[...]).astype(BF16)
            n_ref[r0:r0 + ROW_CHUNK, :] = n
            o_ref[r0:r0 + ROW_CHUNK, :] = x + ff(n)

    @pl.when(j > 0)
    def _():
        for r0 in range(0, tm, ROW_CHUNK):
            o_ref[r0:r0 + ROW_CHUNK, :] += ff(n_ref[r0:r0 + ROW_CHUNK, :])


def _mlp(x2, norm_w, w_up, w_down, tm, tf):
    m, d = x2.shape
    f = w_up.shape[1]
    return pl.pallas_call(
        _mlp_kernel,
        grid=(m // tm, f // tf),
        in_specs=[
            pl.BlockSpec((tm, d), lambda i, j: (i, 0)),
            pl.BlockSpec((1, d), lambda i, j: (0, 0)),
            pl.BlockSpec((d, tf), lambda i, j: (0, j)),
            pl.BlockSpec((tf, d), lambda i, j: (j, 0)),
        ],
        out_specs=pl.BlockSpec((tm, d), lambda i, j: (i, 0)),
        out_shape=jax.ShapeDtypeStruct((m, d), F32),
        scratch_shapes=[pltpu.VMEM((tm, d), BF16)],
        compiler_params=_cparams(("parallel", "arbitrary")),
        name="mlp",
    )(x2, norm_w, w_up, w_down)


def _lambda_init(layer_idx):
    return 0.8 - 0.6 * math.exp(-0.3 * layer_idx)


def kernel(x, positions, attn_norm_w, w_in, gla_gate_w2, gla_gate_b, gla_out_norm_w,
           diff_q_norm_w, diff_k_norm_w, diff_lambda_q1, diff_lambda_k1, diff_lambda_q2,
           diff_lambda_k2, diff_subln_w, w_out, mlp_norm_w, w_up, w_down):
    batch, seq, d = x.shape
    depth = w_in.shape[0]
    m = batch * seq
    assert d == GLA_WIDTH + DIFF_WIDTH and seq % GLA_CHUNK == 0 and seq % Q_TILE == 0
    tm_proj = min(512, m)
    tm_out = min(512, m)
    tm_mlp = min(512, m)

    half = DIFF_HEAD_DIM // 2
    inv_freq = ROPE_THETA ** (-jnp.arange(0, DIFF_HEAD_DIM, 2, dtype=F32) / DIFF_HEAD_DIM)
    ang = (positions.astype(F32)[..., None] * inv_freq).reshape(m, half)
    cos, sin = jnp.cos(ang), jnp.sin(ang)

    wstack_np, level_np = _gla_constants()
    wstack = jnp.asarray(wstack_np, dtype=BF16)
    level = jnp.asarray(level_np)

    sizes = (GLA_KEY_WIDTH, GLA_KEY_WIDTH, GLA_WIDTH, GLA_WIDTH, GLA_GATE_RANK,
             DIFF_WIDTH, DIFF_WIDTH, DIFF_WIDTH)
    offs = np.concatenate([[0], np.cumsum(sizes)])
    lr0, lr1 = int(offs[4]), int(offs[5])

    w_in_parts = _w_in_parts(w_in, 0, (lr0, lr1))

    x2 = x.reshape(m, d)
    for layer in range(depth):
        w2p = jnp.pad(gla_gate_w2[layer], ((0, V7X_LANES - GLA_GATE_RANK), (0, 0))).astype(BF16)
        gate_b = gla_gate_b[layer][None, :]

        qnw = diff_q_norm_w[layer][None, :]
        knw = diff_k_norm_w[layer][None, :]
        proj, lr, b2 = _in_proj(x2, attn_norm_w[layer][None, :], *w_in_parts, cos, sin,
                                qnw, jnp.roll(qnw, half, axis=1), knw, jnp.roll(knw, half, axis=1),
                                w2p, gate_b, tm_proj)
        o_gla, w_out_b = _gla(proj, b2, lr, w2p, gate_b, wstack, level, w_out, layer, batch, seq)
        o_diff, w_up_b, w_down_b, w_in_parts = _diff_attn(
            proj, qnw, knw,
            diff_lambda_q1[layer][None, :], diff_lambda_k1[layer][None, :],
            diff_lambda_q2[layer][None, :], diff_lambda_k2[layer][None, :],
            w_up, w_down, w_in, (lr0, lr1), layer, batch, seq, _lambda_init(layer))
        x2 = _out_proj(x2, o_gla, proj, o_diff, gla_out_norm_w[layer][None, :],
                       diff_subln_w[layer][None, :], w_out_b, tm_out, _lambda_init(layer))
        x2 = _mlp(x2, mlp_norm_w[layer][None, :], w_up_b, w_down_b, tm_mlp, 2048)
    return x2.reshape(batch, seq, d)
```
